```python
import jax, jax.numpy as jnp
from jax import lax
import numpy as np

D_MODEL = 4096
BATCH = 4
SEQ = 2048
DEPTH = 2
DEC_BATCH = 128
DEC_SEQ = 8
PAST_LEN = 16384
PAGE_SIZE = 128

M_HEADS = 8
M_DK = D_MODEL // 16
M_DV = D_MODEL // M_HEADS
M_CHUNK = 64
H_EXPAND = 128
H_HEADS = D_MODEL // H_EXPAND
H_DK = H_EXPAND
H_DV = D_MODEL // H_HEADS
H_CHUNK = 32
P_HEADS = 8
P_NKEYS = 128
P_EXPERTS = P_NKEYS * P_NKEYS
P_TOPK = 16
P_DKEY = 256
P_BLOCK = 64
LN_EPS = 1e-5
DN_ALPHA = (2 * DEPTH) ** 0.25
DN_BETA = (8 * DEPTH) ** -0.25
N_A = (DEPTH + 1) // 2
N_B = DEPTH // 2

kernel_name = 'hybrid_mlstm_hgrn2_peer_step'


def layer_norm(x, g, b):
    xf = x.astype(jnp.float32)
    mu = jnp.mean(xf, axis=-1, keepdims=True)
    var = jnp.mean(jnp.square(xf - mu), axis=-1, keepdims=True)
    return ((xf - mu) * lax.rsqrt(var + LN_EPS) * g + b).astype(x.dtype)


def head_layer_norm(x, g):
    mu = jnp.mean(x, axis=-1, keepdims=True)
    var = jnp.mean(jnp.square(x - mu), axis=-1, keepdims=True)
    return (x - mu) * lax.rsqrt(var + LN_EPS) * g


def head_rms_norm(x, g):
    return x * lax.rsqrt(jnp.mean(jnp.square(x), axis=-1, keepdims=True) + LN_EPS) * g


def chunk_len(T, pref):
    return pref if T % pref == 0 else T


def chunked_scan(step, state, seqs, chunk):
    T = seqs[0].shape[2]
    nc = T // chunk

    def split(a):
        a = a.reshape(a.shape[:2] + (nc, chunk) + a.shape[3:])
        return jnp.moveaxis(a, 2, 0)

    state, out = lax.scan(step, state, tuple(split(a) for a in seqs))
    out = jnp.moveaxis(out, 0, 2)
    return state, out.reshape(out.shape[:2] + (T,) + out.shape[4:])


def mlstm_chunk_step(state, inp):
    C, n, m = state
    q, k, v, ig, lf = inp
    L = q.shape[2]
    b = jnp.cumsum(lf, axis=-1)
    causal = jnp.tril(jnp.ones((L, L), dtype=bool))
    dmat = jnp.where(causal, b[..., :, None] - b[..., None, :] + ig[..., None, :], -jnp.inf)
    inter = b + m[..., None]
    m_t = jnp.maximum(inter, jnp.max(dmat, axis=-1))
    w = jnp.exp(dmat - m_t[..., None])
    a_inter = jnp.exp(inter - m_t)
    s = jnp.einsum('bhtd,bhsd->bhts', q, k) * w
    num = a_inter[..., None] * jnp.einsum('bhtd,bhde->bhte', q, C) + jnp.einsum('bhts,bhse->bhte', s, v)
    den = a_inter * jnp.einsum('bhtd,bhd->bht', q, n) + jnp.sum(s, axis=-1)
    h = num / jnp.maximum(jnp.abs(den), jnp.exp(-m_t))[..., None]
    m_new = m_t[..., -1]
    carry_w = jnp.exp(inter[..., -1] - m_new)
    src_w = jnp.exp(b[..., -1:] - b + ig - m_new[..., None])
    C_new = carry_w[..., None, None] * C + jnp.einsum('bhs,bhsd,bhse->bhde', src_w, k, v)
    n_new = carry_w[..., None] * n + jnp.einsum('bhs,bhsd->bhd', src_w, k)
    return (C_new, n_new, m_new), h


def mlstm_mixer(h, w_in, b_gate, norm_g, w_out, state):
    B, T, _ = h.shape
    f32 = jnp.float32
    qk = M_HEADS * M_DK
    hv = M_HEADS * M_DV
    proj = h @ w_in
    q, k, v, o, gates = jnp.split(proj, [qk, 2 * qk, 2 * qk + hv, 2 * qk + 2 * hv], axis=-1)

    def heads(a, d):
        return a.reshape(B, T, M_HEADS, d).transpose(0, 2, 1, 3).astype(f32)

    q = heads(q, M_DK) * (M_DK ** -0.5)
    k = heads(k, M_DK)
    v = heads(v, M_DV)
    gates = gates.astype(f32) + b_gate.astype(f32)
    ig = gates[..., :M_HEADS].transpose(0, 2, 1)
    lf = jax.nn.log_sigmoid(gates[..., M_HEADS:]).transpose(0, 2, 1)
    state = tuple(s.astype(f32) for s in state)
    state, hh = chunked_scan(mlstm_chunk_step, state, (q, k, v, ig, lf), chunk_len(T, M_CHUNK))
    hh = head_layer_norm(hh.transpose(0, 2, 1, 3), norm_g.astype(f32))
    y = jax.nn.sigmoid(o.astype(f32)).reshape(B, T, M_HEADS, M_DV) * hh
    return y.reshape(B, T, hv).astype(h.dtype) @ w_out, state


def hgrn_chunk_step(S, inp):
    q, k, v, g = inp
    L = q.shape[2]
    bc = jnp.cumsum(g, axis=2)
    causal = jnp.tril(jnp.ones((L, L), dtype=bool))[:, :, None]
    rel = jnp.where(causal, bc[:, :, :, None, :] - bc[:, :, None, :, :], -jnp.inf)
    a = jnp.einsum('bhtd,bhsd,bhtsd->bhts', q, k, jnp.exp(rel))
    o = jnp.einsum('bhtd,bhde->bhte', q * jnp.exp(bc), S) + jnp.einsum('bhts,bhse->bhte', a, v)
    b_end = bc[:, :, -1:, :]
    S_new = jnp.exp(b_end[:, :, 0, :])[..., None] * S + jnp.einsum('bhsd,bhse->bhde', k * jnp.exp(b_end - bc), v)
    return S_new, o


def hgrn_mixer(h, w_in, lb, norm_g, w_out, S):
    B, T, _ = h.shape
    f32 = jnp.float32
    proj = h @ w_in
    q, f, i, g = jnp.split(proj, 4, axis=-1)

    def heads(a):
        return a.reshape(B, T, H_HEADS, -1).transpose(0, 2, 1, 3).astype(f32)

    fg = lb + (1.0 - lb) * jax.nn.sigmoid(f.astype(f32))
    k = 1.0 - fg
    logf = jnp.log(fg)
    qa = jax.nn.silu(q.astype(f32))
    S, o = chunked_scan(hgrn_chunk_step, S.astype(f32), (heads(qa), heads(k), heads(i), heads(logf)), chunk_len(T, H_CHUNK))
    o = head_rms_norm(o.transpose(0, 2, 1, 3), norm_g.astype(f32))
    o = o * jax.nn.silu(g.astype(f32)).reshape(B, T, H_HEADS, H_DV)
    return o.reshape(B, T, H_HEADS * H_DV).astype(h.dtype) @ w_out, S


def peer_ffn(h, w_query, sub_keys, u_tab, v_tab):
    B, T, D = h.shape
    x = h.reshape(B * T, D)
    N = x.shape[0]
    q = (x @ w_query).reshape(N, P_HEADS, 2, P_DKEY // 2).astype(jnp.float32)
    s = jnp.einsum('nhpd,hpkd->nhpk', q, sub_keys.astype(jnp.float32))
    s_top, i_top = lax.top_k(s, P_TOPK)
    cand = (s_top[:, :, 0, :, None] + s_top[:, :, 1, None, :]).reshape(N, P_HEADS, P_TOPK * P_TOPK)
    cand_idx = (i_top[:, :, 0, :, None] * P_NKEYS + i_top[:, :, 1, None, :]).reshape(N, P_HEADS, P_TOPK * P_TOPK)
    best, pos = lax.top_k(cand, P_TOPK)
    idx = jnp.take_along_axis(cand_idx, pos, axis=-1)
    gate = jax.nn.softmax(best, axis=-1)
    n_blk = -(-N // P_BLOCK)
    pad = n_blk * P_BLOCK - N
    xp = jnp.pad(x, ((0, pad), (0, 0))).reshape(n_blk, P_BLOCK, D)
    ip = jnp.pad(idx, ((0, pad), (0, 0), (0, 0))).reshape(n_blk, P_BLOCK, P_HEADS, P_TOPK)
    gp = jnp.pad(gate, ((0, pad), (0, 0), (0, 0))).reshape(n_blk, P_BLOCK, P_HEADS, P_TOPK)

    def block(args):
        xb, ib, gb = args
        u = u_tab[ib]
        act = jax.nn.gelu(jnp.einsum('td,thkd->thk', xb, u).astype(jnp.float32), approximate=False) * gb
        return jnp.einsum('thk,thkd->td', act.astype(xb.dtype), v_tab[ib])

    y = lax.map(block, (xp, ip, gp)).reshape(n_blk * P_BLOCK, D)[:N]
    return y.reshape(B, T, D).astype(h.dtype)


def trunk(x, c, mC, mn, mm, hS, ada_w, ada_b, ln_g, ln_b, m_w_in, m_b_gate, m_norm_g, m_w_out,
          h_w_in, h_lower_bounds, h_norm_g, h_w_out, p_w_query, p_sub_keys, p_u, p_v):
    lb_soft = jax.nn.softmax(h_lower_bounds.astype(jnp.float32), axis=0)
    lb_all = jnp.cumsum(lb_soft, axis=0) - lb_soft[:1]
    cs = jax.nn.silu(c)
    outs_C, outs_n, outs_m, outs_S = [], [], [], []
    for li in range(DEPTH):
        mod = (cs @ ada_w[li] + ada_b[li])[:, None, :]
        sh1, sc1, g1, sh2, sc2, g2 = jnp.split(mod, 6, axis=-1)
        hin = x * (1.0 + sc1) + sh1
        j = li // 2
        if li % 2 == 0:
            mix, (C, n, m) = mlstm_mixer(hin, m_w_in[j], m_b_gate[j], m_norm_g[j], m_w_out[j], (mC[j], mn[j], mm[j]))
            outs_C.append(C)
            outs_n.append(n)
            outs_m.append(m)
        else:
            mix, S = hgrn_mixer(hin, h_w_in[j], lb_all[li], h_norm_g[j], h_w_out[j], hS[j])
            outs_S.append(S)
        x = layer_norm(DN_ALPHA * x + g1 * mix, ln_g[li, 0], ln_b[li, 0])
        hin = x * (1.0 + sc2) + sh2
        ff = peer_ffn(hin, p_w_query[li], p_sub_keys[li], p_u[li], p_v[li])
        x = layer_norm(DN_ALPHA * x + g2 * ff, ln_g[li, 1], ln_b[li, 1])
    return x, jnp.stack(outs_C), jnp.stack(outs_n), jnp.stack(outs_m), jnp.stack(outs_S)


def setup_inputs(seed: int = 0) -> dict:
    key = jax.random.key(seed)
    ks = jax.random.split(key, 28)
    f32 = jnp.float32
    D = D_MODEL
    M_IN = 2 * M_HEADS * M_DK + 2 * M_HEADS * M_DV + 2 * M_HEADS

    def nrm(k, shape, s):
        return jax.random.normal(k, shape, f32) * s

    return {
        'x_prompt': nrm(ks[0], (BATCH, SEQ, D), 1.0),
        'x_sample': nrm(ks[1], (DEC_BATCH, DEC_SEQ, D), 1.0),
        'c_prompt': nrm(ks[2], (BATCH, D), 1.0),
        'c_sample': nrm(ks[3], (DEC_BATCH, D), 1.0),
        'state_mlstm_C': nrm(ks[4], (N_A, DEC_BATCH, M_HEADS, M_DK, M_DV), 0.05),
        'state_mlstm_n': nrm(ks[5], (N_A, DEC_BATCH, M_HEADS, M_DK), 0.5),
        'state_mlstm_m': jax.random.uniform(ks[6], (N_A, DEC_BATCH, M_HEADS), f32, 0.0, 2.0),
        'state_hgrn_S': nrm(ks[7], (N_B, DEC_BATCH, H_HEADS, H_DK, H_DV), 0.1),
        'ada_w': nrm(ks[8], (DEPTH, D, 6 * D), 0.5 * D ** -0.5),
        'ada_b': nrm(ks[9], (DEPTH, 6 * D), 0.02),
        'ln_g': 1.0 + nrm(ks[10], (DEPTH, 2, D), 0.02),
        'ln_b': nrm(ks[11], (DEPTH, 2, D), 0.02),
        'm_w_in': nrm(ks[12], (N_A, D, M_IN), D ** -0.5),
        'm_b_gate': jnp.concatenate([-1.0 + nrm(ks[13], (N_A, M_HEADS), 0.1),
                                     3.0 + nrm(ks[14], (N_A, M_HEADS), 0.5)], axis=-1),
        'm_norm_g': 1.0 + nrm(ks[15], (N_A, M_HEADS, M_DV), 0.02),
        'm_w_out': nrm(ks[16], (N_A, M_HEADS * M_DV, D), DN_BETA * (M_HEADS * M_DV) ** -0.5),
        'h_w_in': nrm(ks[17], (N_B, D, 4 * D), D ** -0.5),
        'h_lower_bounds': nrm(ks[18], (DEPTH, H_HEADS * H_DK), 1.0),
        'h_norm_g': 1.0 + nrm(ks[19], (N_B, H_HEADS, H_DV), 0.02),
        'h_w_out': nrm(ks[20], (N_B, H_HEADS * H_DV, D), DN_BETA * (H_HEADS * H_DV) ** -0.5),
        'p_w_query': nrm(ks[21], (DEPTH, D, P_HEADS * P_DKEY), D ** -0.5),
        'p_sub_keys': nrm(ks[22], (DEPTH, P_HEADS, 2, P_NKEYS, P_DKEY // 2), (P_DKEY // 2) ** -0.5),
        'p_u': nrm(ks[23], (DEPTH, P_EXPERTS, D), D ** -0.5),
        'p_v': nrm(ks[24], (DEPTH, P_EXPERTS, D), DN_BETA * (P_HEADS * P_TOPK) ** -0.5),
    }


def reference(x_prompt, x_sample, c_prompt, c_sample, state_mlstm_C, state_mlstm_n, state_mlstm_m, state_hgrn_S,
              ada_w, ada_b, ln_g, ln_b, m_w_in, m_b_gate, m_norm_g, m_w_out,
              h_w_in, h_lower_bounds, h_norm_g, h_w_out, p_w_query, p_sub_keys, p_u, p_v):
    f32 = jnp.float32
    z_C = jnp.zeros((N_A, BATCH, M_HEADS, M_DK, M_DV), f32)
    z_n = jnp.zeros((N_A, BATCH, M_HEADS, M_DK), f32)
    z_m = jnp.zeros((N_A, BATCH, M_HEADS), f32)
    z_S = jnp.zeros((N_B, BATCH, H_HEADS, H_DK, H_DV), f32)
    y_prompt, p_C, p_n, p_m, p_S = trunk(
        x_prompt, c_prompt, z_C, z_n, z_m, z_S, ada_w, ada_b, ln_g, ln_b, m_w_in, m_b_gate, m_norm_g, m_w_out,
        h_w_in, h_lower_bounds, h_norm_g, h_w_out, p_w_query, p_sub_keys, p_u, p_v)
    y_sample, s_C, s_n, s_m, s_S = trunk(
        x_sample, c_sample, state_mlstm_C, state_mlstm_n, state_mlstm_m, state_hgrn_S,
        ada_w, ada_b, ln_g, ln_b, m_w_in, m_b_gate, m_norm_g, m_w_out,
        h_w_in, h_lower_bounds, h_norm_g, h_w_out, p_w_query, p_sub_keys, p_u, p_v)
    return (y_prompt, y_sample, p_C, p_n, p_m, p_S, s_C, s_n, s_m, s_S)
```

```python
import functools

import jax
import jax.numpy as jnp
from jax import lax
from jax.experimental import pallas as pl
from jax.experimental.pallas import tpu as pltpu

F32 = jnp.float32
BF16 = jnp.bfloat16
HIGHEST = lax.Precision.HIGHEST

LN_EPS = 1e-5
P_TOPK = 16
M_CHUNK_PREF = 64
H_CHUNK_PREF = 32
LANES = 128
SUBLANES = 8
VMEM_LIMIT = 56 * 1024 * 1024

_NT = (((1,), (1,)), ((), ()))
_TN = (((0,), (0,)), ((), ()))


def _params(sem):
    return pltpu.CompilerParams(dimension_semantics=sem, vmem_limit_bytes=VMEM_LIMIT)


def _pick(n, pref):
    t = min(pref, n)
    while n % t:
        t //= 2
    return t


def _matmul_kernel(a_ref, w_ref, o_ref, acc_ref):
    k = pl.program_id(2)

    @pl.when(k == 0)
    def _():
        acc_ref[...] = jnp.zeros_like(acc_ref)

    acc_ref[...] += jnp.dot(a_ref[...].astype(BF16), w_ref[...].astype(BF16), preferred_element_type=F32)

    @pl.when(k == pl.num_programs(2) - 1)
    def _():
        o_ref[...] = acc_ref[...]


def _matmul(a, w, layer, n_cols=None):
    m, kdim = a.shape
    n = n_cols if n_cols is not None else w.shape[2]
    tm, tn, tk = _pick(m, 1024), _pick(n, 1024), _pick(kdim, 1024)
    return pl.pallas_call(
        _matmul_kernel,
        grid=(m // tm, n // tn, kdim // tk),
        in_specs=[pl.BlockSpec((tm, tk), lambda i, j, k: (i, k)),
                  pl.BlockSpec((None, tk, tn), lambda i, j, k: (layer, k, j))],
        out_specs=pl.BlockSpec((tm, tn), lambda i, j, k: (i, j)),
        out_shape=jax.ShapeDtypeStruct((m, n), F32),
        scratch_shapes=[pltpu.VMEM((tm, tn), F32)],
        compiler_params=_params(("parallel", "parallel", "arbitrary")),
        name="matmul",
    )(a, w)


def _ada_kernel(c_ref, w_ref, b_ref, o_ref, acc_ref):
    k = pl.program_id(2)

    @pl.when(k == 0)
    def _():
        acc_ref[...] = jnp.zeros_like(acc_ref)

    cs = jax.nn.silu(c_ref[...])
    acc_ref[...] += jnp.dot(cs.astype(BF16), w_ref[...].astype(BF16), preferred_element_type=F32)

    @pl.when(k == pl.num_programs(2) - 1)
    def _():
        o_ref[...] = acc_ref[...] + b_ref[...]


def _ada(c_all, ada_w, ada_b):
    depth, d, n = ada_w.shape
    rows = c_all.shape[0]
    tn, tk = _pick(n, 2048), _pick(d, 1024)
    return pl.pallas_call(
        _ada_kernel,
        grid=(depth, n // tn, d // tk),
        in_specs=[pl.BlockSpec((rows, tk), lambda l, j, k: (0, k)),
                  pl.BlockSpec((None, tk, tn), lambda l, j, k: (l, k, j)),
                  pl.BlockSpec((None, 1, tn), lambda l, j, k: (l, 0, j))],
        out_specs=pl.BlockSpec((None, rows, tn), lambda l, j, k: (l, 0, j)),
        out_shape=jax.ShapeDtypeStruct((depth, rows, n), F32),
        scratch_shapes=[pltpu.VMEM((rows, tn), F32)],
        compiler_params=_params(("parallel", "parallel", "arbitrary")),
        name="ada",
    )(c_all, ada_w, ada_b.reshape(depth, 1, n))


def _seq_blocks(x3):
    s, r, d = x3.shape
    if r >= 128:
        return 1, _pick(r, 128)
    return _pick(s, max(1, 128 // r)), r


def _mod_spec(layer, chunk, sb, seq_off, d):
    return pl.BlockSpec((None, sb, 1, d), lambda s, r: (layer, seq_off // sb + s, 0, chunk))


def _modulate_kernel(x_ref, sc_ref, sh_ref, o_ref):
    o_ref[...] = x_ref[...] * (1.0 + sc_ref[...]) + sh_ref[...]


def _modulate(x3, mod, layer, seq_off):
    s, r, d = x3.shape
    sb, rb = _seq_blocks(x3)
    xs = pl.BlockSpec((sb, rb, d), lambda i, j: (i, j, 0))
    return pl.pallas_call(
        _modulate_kernel,
        grid=(s // sb, r // rb),
        in_specs=[xs, _mod_spec(layer, 1, sb, seq_off, d), _mod_spec(layer, 0, sb, seq_off, d)],
        out_specs=xs,
        out_shape=jax.ShapeDtypeStruct(x3.shape, F32),
        compiler_params=_params(("parallel", "parallel")),
        name="modulate",
    )(x3, mod, mod)


def _res_ln_kernel(x_ref, y_ref, g_ref, lg_ref, lb_ref, *rest, alpha, modulate):
    z = alpha * x_ref[...] + g_ref[...] * y_ref[...]
    mu = jnp.mean(z, axis=-1, keepdims=True)
    zc = z - mu
    var = jnp.mean(zc * zc, axis=-1, keepdims=True)
    xn = zc * lax.rsqrt(var + LN_EPS) * lg_ref[...] + lb_ref[...]
    if modulate:
        sc_ref, sh_ref, xo_ref, ho_ref = rest
        xo_ref[...] = xn
        ho_ref[...] = xn * (1.0 + sc_ref[...]) + sh_ref[...]
    else:
        (xo_ref,) = rest
        xo_ref[...] = xn


def _res_ln(x3, y3, mod, ln_g, ln_b, layer, sub, seq_off, alpha, next_mod):
    s, r, d = x3.shape
    sb, rb = _seq_blocks(x3)
    xs = pl.BlockSpec((sb, rb, d), lambda i, j: (i, j, 0))
    ln_spec = pl.BlockSpec((None, 1, d), lambda i, j: (2 * layer + sub, 0, 0))
    lg = ln_g.reshape(-1, 1, d)
    lb = ln_b.reshape(-1, 1, d)
    in_specs = [xs, xs, _mod_spec(layer, 2 + 3 * sub, sb, seq_off, d), ln_spec, ln_spec]
    args = [x3, y3, mod, lg, lb]
    out_shape = [jax.ShapeDtypeStruct(x3.shape, F32)]
    out_specs = [xs]
    if next_mod is not None:
        nl, sc_chunk, sh_chunk = next_mod
        in_specs += [_mod_spec(nl, sc_chunk, sb, seq_off, d), _mod_spec(nl, sh_chunk, sb, seq_off, d)]
        args += [mod, mod]
        out_shape.append(jax.ShapeDtypeStruct(x3.shape, F32))
        out_specs.append(xs)
    out = pl.pallas_call(
        functools.partial(_res_ln_kernel, alpha=alpha, modulate=next_mod is not None),
        grid=(s // sb, r // rb),
        in_specs=in_specs, out_specs=out_specs, out_shape=out_shape,
        compiler_params=_params(("parallel", "parallel")),
        name="res_ln",
    )(*args)
    return (out[0], out[1]) if next_mod is not None else (out[0], None)


def _eye(n, dtype=F32):
    r = lax.broadcasted_iota(jnp.int32, (n, n), 0)
    c = lax.broadcasted_iota(jnp.int32, (n, n), 1)
    return (r == c).astype(dtype)


def _tril(n):
    r = lax.broadcasted_iota(jnp.int32, (n, n), 0)
    c = lax.broadcasted_iota(jnp.int32, (n, n), 1)
    return (r >= c).astype(F32)


def _transpose_exact(x):
    return lax.dot_general(_eye(x.shape[1]), x, _NT, precision=HIGHEST, preferred_element_type=F32)


def _mlstm_kernel(q_ref, k_ref, v_ref, o_ref, g_ref, bg_ref, ng_ref, c0_ref, n0_ref, m0_ref,
                  y_ref, c_ref, n_ref, m_ref, *, heads, dk):
    h = pl.program_id(1)

    @pl.when(pl.program_id(2) == 0)
    def _():
        c_ref[...] = c0_ref[...]
        n_ref[...] = n0_ref[...]
        m_ref[...] = m0_ref[...]

    length = q_ref.shape[0]
    gates = g_ref[...] + bg_ref[...]
    lane = lax.broadcasted_iota(jnp.int32, gates.shape, 1)
    ig_col = jnp.sum(jnp.where(lane == h, gates, 0.0), axis=1, keepdims=True)
    lf_col = jax.nn.log_sigmoid(jnp.sum(jnp.where(lane == h + heads, gates, 0.0), axis=1, keepdims=True))
    ig_b = jnp.broadcast_to(ig_col, (length, LANES))
    lf_b = jnp.broadcast_to(lf_col, (length, LANES))
    tri = _tril(length)
    b_b = jnp.dot(tri, lf_b, precision=HIGHEST, preferred_element_type=F32)
    b_col = b_b[:, 0:1]
    sel = (lax.broadcasted_iota(jnp.int32, (SUBLANES, LANES), 1) == 0).astype(F32)
    b_row = lax.dot_general(sel, b_b, _NT, precision=HIGHEST, preferred_element_type=F32)[0:1]
    ig_row = lax.dot_general(sel, ig_b, _NT, precision=HIGHEST, preferred_element_type=F32)[0:1]

    m_prev = m_ref[...]
    causal = tri > 0.5
    dmat = jnp.where(causal, b_col - b_row + ig_row, -jnp.inf)
    inter = b_col + m_prev
    m_t = jnp.maximum(inter, jnp.max(dmat, axis=1, keepdims=True))
    w = jnp.exp(dmat - m_t)
    a_inter = jnp.exp(inter - m_t)

    q = q_ref[...] * (dk ** -0.5)
    k = k_ref[...]
    v = v_ref[...]
    qb, kb, vb = q.astype(BF16), k.astype(BF16), v.astype(BF16)
    cmat = c_ref[...]
    nvec = n_ref[...]
    s = lax.dot_general(qb, kb, _NT, preferred_element_type=F32) * w
    num = a_inter * jnp.dot(qb, cmat.astype(BF16), preferred_element_type=F32) \
        + jnp.dot(s.astype(BF16), vb, preferred_element_type=F32)
    den = a_inter * jnp.sum(q * nvec, axis=1, keepdims=True) + jnp.sum(s, axis=1, keepdims=True)
    hh = num / jnp.maximum(jnp.abs(den), jnp.exp(-m_t))

    mu = jnp.mean(hh, axis=1, keepdims=True)
    hc = hh - mu
    var = jnp.mean(hc * hc, axis=1, keepdims=True)
    hn = hc * lax.rsqrt(var + LN_EPS) * ng_ref[...]
    y_ref[...] = jax.nn.sigmoid(o_ref[...]) * hn

    m_new = m_t[length - 1:length, :]
    carry = jnp.exp(inter[length - 1:length, :] - m_new)
    src = jnp.exp(b_col[length - 1:length, :] - b_col + ig_col - m_new)
    kt = k * src
    c_ref[...] = carry * cmat + lax.dot_general(kt.astype(BF16), vb, _TN, preferred_element_type=F32)
    n_ref[...] = carry * nvec + jnp.sum(kt, axis=0, keepdims=True)
    m_ref[...] = m_new


def _mlstm(proj, gates, b_gate, norm_g, c0, n0, m0, batch, seq):
    _, heads, dk, dv = c0.shape
    chunk = _pick(seq, 256) if seq % M_CHUNK_PREF == 0 else seq
    nc = seq // chunk
    kq = heads
    kv = 2 * heads * dk // dv
    ko = kv + heads

    def rows(b, h, c):
        return b * nc + c

    state_c = pl.BlockSpec((None, None, dk, dv), lambda b, h, c: (b, h, 0, 0))
    state_n = pl.BlockSpec((None, None, 1, dk), lambda b, h, c: (b, h, 0, 0))
    state_m = pl.BlockSpec((None, None, 1, 1), lambda b, h, c: (b, h, 0, 0))
    y, c1, n1, m1 = pl.pallas_call(
        functools.partial(_mlstm_kernel, heads=heads, dk=dk),
        grid=(batch, heads, nc),
        in_specs=[pl.BlockSpec((chunk, dk), lambda b, h, c: (rows(b, h, c), h)),
                  pl.BlockSpec((chunk, dk), lambda b, h, c: (rows(b, h, c), kq + h)),
                  pl.BlockSpec((chunk, dv), lambda b, h, c: (rows(b, h, c), kv + h)),
                  pl.BlockSpec((chunk, dv), lambda b, h, c: (rows(b, h, c), ko + h)),
                  pl.BlockSpec((chunk, LANES), lambda b, h, c: (rows(b, h, c), 0)),
                  pl.BlockSpec((1, LANES), lambda b, h, c: (0, 0)),
                  pl.BlockSpec((None, 1, dv), lambda b, h, c: (h, 0, 0)),
                  state_c, state_n, state_m],
        out_specs=[pl.BlockSpec((chunk, dv), lambda b, h, c: (rows(b, h, c), h)), state_c, state_n, state_m],
        out_shape=[jax.ShapeDtypeStruct((batch * seq, heads * dv), F32),
                   jax.ShapeDtypeStruct((batch, heads, dk, dv), F32),
                   jax.ShapeDtypeStruct((batch, heads, 1, dk), F32),
                   jax.ShapeDtypeStruct((batch, heads, 1, 1), F32)],
        compiler_params=_params(("parallel", "parallel", "arbitrary")),
        name="mlstm",
    )(proj, proj, proj, proj, gates, b_gate, norm_g.reshape(heads, 1, dv),
      c0, n0.reshape(batch, heads, 1, dk), m0.reshape(batch, heads, 1, 1))
    return y, c1, n1.reshape(batch, heads, dk), m1.reshape(batch, heads)


def _hgrn_kernel(q_ref, f_ref, i_ref, g_ref, lbw_ref, ng_ref, s0_ref, y_ref, s_ref, st_ref,
                 *, layer, chunk, hb):
    tb = pl.program_id(2)
    dh = LANES
    eye = _eye(dh)

    @pl.when(tb == 0)
    def _():
        for hh in range(hb):
            st_ref[hh] = lax.dot_general(eye, s0_ref[hh], _NT, precision=HIGHEST, preferred_element_type=F32)

    lbw = lbw_ref[...]
    e = jnp.exp(lbw - jnp.max(lbw, axis=0, keepdims=True))
    soft = e / jnp.sum(e, axis=0, keepdims=True)
    lb_all = jnp.sum(soft[0:layer + 1], axis=0, keepdims=True) - soft[0:1]

    tri = _tril(chunk)
    n_chunks = q_ref.shape[0] // chunk

    def chunk_body(ci, carry):
        r0 = pl.multiple_of(ci * chunk, chunk)
        for hh in range(hb):
            cs = slice(hh * dh, (hh + 1) * dh)
            lb = lb_all[:, cs]
            fg = lb + (1.0 - lb) * jax.nn.sigmoid(f_ref[pl.ds(r0, chunk), cs])
            kk = 1.0 - fg
            logf = jnp.log(fg)
            qx = q_ref[pl.ds(r0, chunk), cs]
            qa = qx * jax.nn.sigmoid(qx)
            v = i_ref[pl.ds(r0, chunk), cs]
            bc = jnp.dot(tri, logf, precision=HIGHEST, preferred_element_type=F32)
            st = st_ref[hh]
            o = lax.dot_general((qa * jnp.exp(bc)).astype(BF16), st.astype(BF16), _NT, preferred_element_type=F32)
            rows_out = []
            for t in range(chunk):
                nb = (t // SUBLANES + 1) * SUBLANES
                srow = lax.broadcasted_iota(jnp.int32, (nb, dh), 0)
                rel = jnp.where(srow <= t, bc[t:t + 1, :] - bc[0:nb, :], -jnp.inf)
                p = qa[t:t + 1, :] * kk[0:nb, :] * jnp.exp(rel)
                a = jnp.sum(p, axis=1, keepdims=True)
                rows_out.append(jnp.sum(a * v[0:nb, :], axis=0, keepdims=True))
            o = o + jnp.concatenate(rows_out, axis=0)
            b_end = bc[chunk - 1:chunk, :]
            kt = kk * jnp.exp(b_end - bc)
            st_ref[hh] = st * jnp.exp(b_end) + lax.dot_general(v.astype(BF16), kt.astype(BF16), _TN,
                                                               preferred_element_type=F32)
            on = o * lax.rsqrt(jnp.mean(o * o, axis=1, keepdims=True) + LN_EPS) * ng_ref[:, cs]
            gx = g_ref[pl.ds(r0, chunk), cs]
            y_ref[pl.ds(r0, chunk), cs] = on * (gx * jax.nn.sigmoid(gx))
        return carry

    lax.fori_loop(0, n_chunks, chunk_body, 0)

    @pl.when(tb == pl.num_programs(2) - 1)
    def _():
        for hh in range(hb):
            s_ref[hh] = lax.dot_general(eye, st_ref[hh], _NT, precision=HIGHEST, preferred_element_type=F32)


def _hgrn(proj, lower_bounds, layer, norm_g, s0, batch, seq):
    _, heads, dk, dv = s0.shape
    assert dk == LANES and dv == LANES
    d = heads * dk
    chunk = H_CHUNK_PREF if seq % H_CHUNK_PREF == 0 else seq
    tb = _pick(seq, 256) if seq % H_CHUNK_PREF == 0 else seq
    hb = 1 if tb >= 128 else _pick(heads, 8)
    ng = heads // hb
    nt = seq // tb

    def col(part):
        return pl.BlockSpec((tb, hb * dk), lambda b, h, t: (b * nt + t, part * ng + h))

    state = pl.BlockSpec((None, hb, dk, dv), lambda b, h, t: (b, h, 0, 0))
    depth = lower_bounds.shape[0]
    y, s1 = pl.pallas_call(
        functools.partial(_hgrn_kernel, layer=layer, chunk=chunk, hb=hb),
        grid=(batch, ng, nt),
        in_specs=[col(0), col(1), col(2), col(3),
                  pl.BlockSpec((depth, hb * dk), lambda b, h, t: (0, h)),
                  pl.BlockSpec((1, hb * dv), lambda b, h, t: (0, h)),
                  state],
        out_specs=[pl.BlockSpec((tb, hb * dv), lambda b, h, t: (b * nt + t, h)), state],
        out_shape=[jax.ShapeDtypeStruct((batch * seq, d), F32),
                   jax.ShapeDtypeStruct(s0.shape, F32)],
        scratch_shapes=[pltpu.VMEM((hb, dv, dk), F32)],
        compiler_params=_params(("parallel", "parallel", "arbitrary")),
        name="hgrn",
    )(proj, proj, proj, proj, lower_bounds, norm_g.reshape(1, d), s0)
    return y, s1


def _topk_rows(s, k, payload=None):
    rows = lax.broadcasted_iota(jnp.int32, s.shape, 0)
    vals, picks = [], []
    for _ in range(k):
        m = jnp.max(s, axis=0, keepdims=True)
        pos = jnp.min(jnp.where(s == m, rows, s.shape[0]), axis=0, keepdims=True)
        hit = rows == pos
        vals.append(m)
        picks.append(pos if payload is None else jnp.max(jnp.where(hit, payload, -1), axis=0, keepdims=True))
        s = jnp.where(hit, -jnp.inf, s)
    return jnp.concatenate(vals, axis=0), jnp.concatenate(picks, axis=0)


def _route_kernel(q_ref, keys_ref, idx_ref, gate_ref, *, n_keys):
    half = q_ref.shape[1] // 2
    for g in range(q_ref.shape[0] // LANES):
        cols = slice(g * LANES, (g + 1) * LANES)
        qb = q_ref[cols, :].astype(BF16)
        tops = []
        for p in range(2):
            st = lax.dot_general(keys_ref[p].astype(BF16), qb[:, p * half:(p + 1) * half], _NT,
                                 preferred_element_type=F32)
            tops.append(_topk_rows(st, P_TOPK))
        (s1, i1), (s2, i2) = tops
        cand = jnp.concatenate([s1[a:a + 1] + s2 for a in range(P_TOPK)], axis=0)
        cidx = jnp.concatenate([i1[a:a + 1] * n_keys + i2 for a in range(P_TOPK)], axis=0)
        best, experts = _topk_rows(cand, P_TOPK, payload=cidx)
        ex = jnp.exp(best - best[0:1])
        idx_ref[:, cols] = experts
        gate_ref[:, cols] = ex / jnp.sum(ex, axis=0, keepdims=True)


def _route(q, sub_keys, layer):
    m = q.shape[0]
    _, heads, _, n_keys, half = sub_keys.shape
    tb = _pick(m, 512)
    out = pl.BlockSpec((None, P_TOPK, tb), lambda i, h: (h, 0, i))
    return pl.pallas_call(
        functools.partial(_route_kernel, n_keys=n_keys),
        grid=(m // tb, heads),
        in_specs=[pl.BlockSpec((tb, 2 * half), lambda i, h: (i, h)),
                  pl.BlockSpec((None, None, 2, n_keys, half), lambda i, h: (layer, h, 0, 0, 0))],
        out_specs=[out, out],
        out_shape=[jax.ShapeDtypeStruct((heads, P_TOPK, m), jnp.int32),
                   jax.ShapeDtypeStruct((heads, P_TOPK, m), F32)],
        compiler_params=_params(("parallel", "parallel")),
        name="route",
    )(q, sub_keys)


EVAL_TOKENS = 8


def _pack_table(tab):
    half = tab.shape[-1] // 2
    hi = lax.bitcast_convert_type(tab[..., :half].astype(BF16), jnp.uint16).astype(jnp.uint32)
    lo = lax.bitcast_convert_type(tab[..., half:].astype(BF16), jnp.uint16).astype(jnp.uint32)
    return lax.bitcast_convert_type((hi << 16) | lo, jnp.int32)


def _unpack(words):
    hi = pltpu.bitcast(words & jnp.int32(-65536), F32)
    lo = pltpu.bitcast(words << 16, F32)
    return hi, lo


def _eval_kernel(idx0_ref, idx1_ref, x_ref, gate_ref, u_hbm, v_hbm, y_ref, ubuf, vbuf, usem, vsem,
                 *, layer, n_picks):
    i = pl.program_id(0)
    n_steps = pl.num_programs(0)
    tokens = x_ref.shape[0]
    rows = tokens * n_picks
    half = x_ref.shape[1] // 2

    def issue(idx_ref, slot):
        def body(p, carry):
            e = idx_ref[p // n_picks, p % n_picks]
            pltpu.make_async_copy(u_hbm.at[layer, pl.ds(e, 1)], ubuf.at[slot, pl.ds(p, 1)], usem.at[slot]).start()
            pltpu.make_async_copy(v_hbm.at[layer, pl.ds(e, 1)], vbuf.at[slot, pl.ds(p, 1)], vsem.at[slot]).start()
            return carry
        lax.fori_loop(0, rows, body, 0, unroll=8)

    slot = i % 2

    @pl.when(i == 0)
    def _():
        issue(idx0_ref, 0)

    @pl.when(i + 1 < n_steps)
    def _():
        issue(idx1_ref, 1 - slot)

    pltpu.make_async_copy(ubuf.at[slot], ubuf.at[slot], usem.at[slot]).wait()
    pltpu.make_async_copy(vbuf.at[slot], vbuf.at[slot], vsem.at[slot]).wait()

    gate_t = _transpose_exact(gate_ref[...])
    for t in range(tokens):
        r0 = t * n_picks
        x = x_ref[t:t + 1, :]
        uh, ul = _unpack(ubuf[slot, pl.ds(r0, n_picks), :])
        act = jnp.sum(uh * x[:, :half] + ul * x[:, half:], axis=1, keepdims=True)
        a = 0.5 * act * (1.0 + lax.erf(act * (2.0 ** -0.5))) * gate_t[:, t:t + 1]
        vh, vl = _unpack(vbuf[slot, pl.ds(r0, n_picks), :])
        y_ref[t:t + 1, :half] = jnp.sum(a * vh, axis=0, keepdims=True)
        y_ref[t:t + 1, half:] = jnp.sum(a * vl, axis=0, keepdims=True)


def _peer_eval(x, idx, gate, u_pack, v_pack, layer):
    m, d = x.shape
    n_picks = idx.shape[1]
    tb = EVAL_TOKENS
    n = m // tb
    half = d // 2
    return pl.pallas_call(
        functools.partial(_eval_kernel, layer=layer, n_picks=n_picks),
        grid=(n,),
        in_specs=[pl.BlockSpec((tb, n_picks), lambda i: (i, 0), memory_space=pltpu.SMEM),
                  pl.BlockSpec((tb, n_picks), lambda i: (jnp.minimum(i + 1, n - 1), 0), memory_space=pltpu.SMEM),
                  pl.BlockSpec((tb, d), lambda i: (i, 0)),
                  pl.BlockSpec((tb, n_picks), lambda i: (i, 0)),
                  pl.BlockSpec(memory_space=pl.ANY),
                  pl.BlockSpec(memory_space=pl.ANY)],
        out_specs=pl.BlockSpec((tb, d), lambda i: (i, 0)),
        out_shape=jax.ShapeDtypeStruct((m, d), F32),
        scratch_shapes=[pltpu.VMEM((2, tb * n_picks, half), jnp.int32),
                        pltpu.VMEM((2, tb * n_picks, half), jnp.int32),
                        pltpu.SemaphoreType.DMA((2,)),
                        pltpu.SemaphoreType.DMA((2,))],
        compiler_params=_params(("arbitrary",)),
        name="peer_eval",
    )(idx, idx, x, gate, u_pack, v_pack)


def _peer(h, w_query, sub_keys, u_pack, v_pack, layer):
    m = h.shape[0]
    q = _matmul(h, w_query, layer)
    idx_t, gate_t = _route(q, sub_keys, layer)
    idx = idx_t.reshape(-1, m).T
    gate = gate_t.reshape(-1, m).T
    return _peer_eval(h, idx, gate, u_pack, v_pack, layer)


def _trunk(x, mod, seq_off, mc, mn, mm, hs, ln_g, ln_b, m_w_in, w_gate, b_gate, m_norm_g, m_w_out,
           h_w_in, h_lower_bounds, h_norm_g, h_w_out, p_w_query, p_sub_keys, u_pack, v_pack):
    batch, seq, d = x.shape
    depth = mod.shape[0]
    alpha = (2 * depth) ** 0.25
    rows = batch * seq
    out_c, out_n, out_m, out_s = [], [], [], []
    hin = _modulate(x, mod, 0, seq_off)
    for li in range(depth):
        j = li // 2
        hin2 = hin.reshape(rows, d)
        if li % 2 == 0:
            n_main = m_w_in.shape[2] - 2 * m_norm_g.shape[1]
            proj = _matmul(hin2, m_w_in, j, n_cols=n_main)
            gates = _matmul(hin2, w_gate, j)
            y, c1, n1, m1 = _mlstm(proj, gates, b_gate[j:j + 1], m_norm_g[j], mc[j], mn[j], mm[j], batch, seq)
            mix = _matmul(y, m_w_out, j)
            out_c.append(c1)
            out_n.append(n1)
            out_m.append(m1)
        else:
            proj = _matmul(hin2, h_w_in, j)
            y, s1 = _hgrn(proj, h_lower_bounds, li, h_norm_g[j], hs[j], batch, seq)
            mix = _matmul(y, h_w_out, j)
            out_s.append(s1)
        x, hin = _res_ln(x, mix.reshape(batch, seq, d), mod, ln_g, ln_b, li, 0, seq_off, alpha, (li, 4, 3))
        ff = _peer(hin.reshape(rows, d), p_w_query, p_sub_keys, u_pack, v_pack, li)
        nxt = (li + 1, 1, 0) if li + 1 < depth else None
        x, hin = _res_ln(x, ff.reshape(batch, seq, d), mod, ln_g, ln_b, li, 1, seq_off, alpha, nxt)
    return x, jnp.stack(out_c), jnp.stack(out_n), jnp.stack(out_m), jnp.stack(out_s)


def kernel(x_prompt, x_sample, c_prompt, c_sample, state_mlstm_C, state_mlstm_n, state_mlstm_m, state_hgrn_S, ada_w, ada_b, ln_g, ln_b, m_w_in, m_b_gate, m_norm_g, m_w_out, h_w_in, h_lower_bounds, h_norm_g, h_w_out, p_w_query, p_sub_keys, p_u, p_v):
    depth, d, _ = ada_w.shape
    n_a, m_heads, m_dv = m_norm_g.shape
    n_b = h_norm_g.shape[0]
    batch = x_prompt.shape[0]
    dec_batch = x_sample.shape[0]

    n_seq = dec_batch + batch
    pad = -n_seq % SUBLANES
    c_all = jnp.concatenate([c_sample, c_prompt, jnp.zeros((pad, d), F32)], axis=0)
    mod = _ada(c_all, ada_w, ada_b).reshape(depth, n_seq + pad, 1, 6 * d)

    n_main = m_w_in.shape[2] - 2 * m_heads
    w_gate = jnp.pad(m_w_in[:, :, n_main:], ((0, 0), (0, 0), (0, LANES - 2 * m_heads)))
    b_gate = jnp.pad(m_b_gate, ((0, 0), (0, LANES - 2 * m_heads)))
    u_pack = _pack_table(p_u)
    v_pack = _pack_table(p_v)

    m_dk = state_mlstm_C.shape[3]
    _, _, h_heads, h_dk, h_dv = state_hgrn_S.shape
    z_c = jnp.zeros((n_a, batch, m_heads, m_dk, m_dv), F32)
    z_n = jnp.zeros((n_a, batch, m_heads, m_dk), F32)
    z_m = jnp.zeros((n_a, batch, m_heads), F32)
    z_s = jnp.zeros((n_b, batch, h_heads, h_dk, h_dv), F32)

    shared = (ln_g, ln_b, m_w_in, w_gate, b_gate, m_norm_g, m_w_out, h_w_in, h_lower_bounds, h_norm_g, h_w_out,
              p_w_query, p_sub_keys, u_pack, v_pack)
    y_p, p_c, p_n, p_m, p_s = _trunk(x_prompt, mod, dec_batch, z_c, z_n, z_m, z_s, *shared)
    y_s, s_c, s_n, s_m, s_s = _trunk(x_sample, mod, 0, state_mlstm_C, state_mlstm_n, state_mlstm_m, state_hgrn_S,
                                     *shared)
    return (y_p, y_s, p_c, p_n, p_m, p_s, s_c, s_n, s_m, s_s)
```

```python
import functools

import jax
import jax.numpy as jnp
from jax import lax
from jax.experimental import pallas as pl
from jax.experimental.pallas import tpu as pltpu

F32 = jnp.float32
BF16 = jnp.bfloat16
HIGHEST = lax.Precision.HIGHEST

LN_EPS = 1e-5
P_TOPK = 16
M_CHUNK_PREF = 64
H_CHUNK_PREF = 32
LANES = 128
SUBLANES = 8
VMEM_LIMIT = 56 * 1024 * 1024

_NT = (((1,), (1,)), ((), ()))
_TN = (((0,), (0,)), ((), ()))


def _params(sem):
    return pltpu.CompilerParams(dimension_semantics=sem, vmem_limit_bytes=VMEM_LIMIT)


def _pick(n, pref):
    t = min(pref, n)
    while n % t:
        t //= 2
    return t


def _matmul_kernel(a_ref, w_ref, o_ref, acc_ref):
    k = pl.program_id(2)

    @pl.when(k == 0)
    def _():
        acc_ref[...] = jnp.zeros_like(acc_ref)

    acc_ref[...] += jnp.dot(a_ref[...].astype(BF16), w_ref[...].astype(BF16), preferred_element_type=F32)

    @pl.when(k == pl.num_programs(2) - 1)
    def _():
        o_ref[...] = acc_ref[...]


def _matmul(a, w, layer, n_cols=None):
    m, kdim = a.shape
    n = n_cols if n_cols is not None else w.shape[2]
    tm, tn, tk = _pick(m, 2048 if a.dtype == BF16 else 1024), _pick(n, 1024), _pick(kdim, 1024)
    return pl.pallas_call(
        _matmul_kernel,
        grid=(m // tm, n // tn, kdim // tk),
        in_specs=[pl.BlockSpec((tm, tk), lambda i, j, k: (i, k)),
                  pl.BlockSpec((None, tk, tn), lambda i, j, k: (layer, k, j))],
        out_specs=pl.BlockSpec((tm, tn), lambda i, j, k: (i, j)),
        out_shape=jax.ShapeDtypeStruct((m, n), F32),
        scratch_shapes=[pltpu.VMEM((tm, tn), F32)],
        compiler_params=_params(("parallel", "parallel", "arbitrary")),
        name="matmul",
    )(a, w)


def _ada_kernel(c_ref, w_ref, b_ref, o_ref, acc_ref):
    k = pl.program_id(2)

    @pl.when(k == 0)
    def _():
        acc_ref[...] = jnp.zeros_like(acc_ref)

    cs = jax.nn.silu(c_ref[...])
    acc_ref[...] += jnp.dot(cs.astype(BF16), w_ref[...].astype(BF16), preferred_element_type=F32)

    @pl.when(k == pl.num_programs(2) - 1)
    def _():
        o_ref[...] = acc_ref[...] + b_ref[...]


def _ada(c_all, ada_w, ada_b):
    depth, d, n = ada_w.shape
    rows = c_all.shape[0]
    tn, tk = _pick(n, 2048), _pick(d, 1024)
    return pl.pallas_call(
        _ada_kernel,
        grid=(depth, n // tn, d // tk),
        in_specs=[pl.BlockSpec((rows, tk), lambda l, j, k: (0, k)),
                  pl.BlockSpec((None, tk, tn), lambda l, j, k: (l, k, j)),
                  pl.BlockSpec((None, 1, tn), lambda l, j, k: (l, 0, j))],
        out_specs=pl.BlockSpec((None, rows, tn), lambda l, j, k: (l, 0, j)),
        out_shape=jax.ShapeDtypeStruct((depth, rows, n), F32),
        scratch_shapes=[pltpu.VMEM((rows, tn), F32)],
        compiler_params=_params(("parallel", "parallel", "arbitrary")),
        name="ada",
    )(c_all, ada_w, ada_b.reshape(depth, 1, n))


def _seq_blocks(x3):
    s, r, d = x3.shape
    if r >= 128:
        return 1, _pick(r, 128)
    return _pick(s, max(1, 128 // r)), r


def _mod_spec(layer, chunk, sb, seq_off, d):
    return pl.BlockSpec((None, sb, 1, d), lambda s, r: (layer, seq_off // sb + s, 0, chunk))


def _store_rows(o_ref, h):
    o_ref[...] = h.reshape(o_ref.shape).astype(o_ref.dtype)


def _rows_spec(sb, rb, r, d):
    return pl.BlockSpec((sb * rb, d), lambda i, j: (i * (r // rb) + j, 0))


def _modulate_kernel(x_ref, sc_ref, sh_ref, o_ref):
    _store_rows(o_ref, x_ref[...] * (1.0 + sc_ref[...]) + sh_ref[...])


def _modulate(x3, mod, layer, seq_off):
    s, r, d = x3.shape
    sb, rb = _seq_blocks(x3)
    xs = pl.BlockSpec((sb, rb, d), lambda i, j: (i, j, 0))
    return pl.pallas_call(
        _modulate_kernel,
        grid=(s // sb, r // rb),
        in_specs=[xs, _mod_spec(layer, 1, sb, seq_off, d), _mod_spec(layer, 0, sb, seq_off, d)],
        out_specs=_rows_spec(sb, rb, r, d),
        out_shape=jax.ShapeDtypeStruct((s * r, d), BF16),
        compiler_params=_params(("parallel", "parallel")),
        name="modulate",
    )(x3, mod, mod)


def _res_ln_kernel(x_ref, y_ref, g_ref, lg_ref, lb_ref, *rest, alpha, modulate):
    z = alpha * x_ref[...] + g_ref[...] * y_ref[...]
    mu = jnp.mean(z, axis=-1, keepdims=True)
    zc = z - mu
    var = jnp.mean(zc * zc, axis=-1, keepdims=True)
    xn = zc * lax.rsqrt(var + LN_EPS) * lg_ref[...] + lb_ref[...]
    if modulate:
        sc_ref, sh_ref, xo_ref, ho_ref = rest
        xo_ref[...] = xn
        _store_rows(ho_ref, xn * (1.0 + sc_ref[...]) + sh_ref[...])
    else:
        (xo_ref,) = rest
        xo_ref[...] = xn


def _res_ln(x3, y3, mod, ln_g, ln_b, layer, sub, seq_off, alpha, next_mod):
    s, r, d = x3.shape
    sb, rb = _seq_blocks(x3)
    xs = pl.BlockSpec((sb, rb, d), lambda i, j: (i, j, 0))
    ln_spec = pl.BlockSpec((None, 1, d), lambda i, j: (2 * layer + sub, 0, 0))
    lg = ln_g.reshape(-1, 1, d)
    lb = ln_b.reshape(-1, 1, d)
    in_specs = [xs, xs, _mod_spec(layer, 2 + 3 * sub, sb, seq_off, d), ln_spec, ln_spec]
    args = [x3, y3, mod, lg, lb]
    out_shape = [jax.ShapeDtypeStruct(x3.shape, F32)]
    out_specs = [xs]
    if next_mod is not None:
        nl, sc_chunk, sh_chunk = next_mod
        in_specs += [_mod_spec(nl, sc_chunk, sb, seq_off, d), _mod_spec(nl, sh_chunk, sb, seq_off, d)]
        args += [mod, mod]
        out_shape.append(jax.ShapeDtypeStruct((s * r, d), BF16))
        out_specs.append(_rows_spec(sb, rb, r, d))
    out = pl.pallas_call(
        functools.partial(_res_ln_kernel, alpha=alpha, modulate=next_mod is not None),
        grid=(s // sb, r // rb),
        in_specs=in_specs, out_specs=out_specs, out_shape=out_shape,
        compiler_params=_params(("parallel", "parallel")),
        name="res_ln",
    )(*args)
    return (out[0], out[1]) if next_mod is not None else (out[0], None)


def _eye(n, dtype=F32):
    r = lax.broadcasted_iota(jnp.int32, (n, n), 0)
    c = lax.broadcasted_iota(jnp.int32, (n, n), 1)
    return (r == c).astype(dtype)


def _tril(n):
    r = lax.broadcasted_iota(jnp.int32, (n, n), 0)
    c = lax.broadcasted_iota(jnp.int32, (n, n), 1)
    return (r >= c).astype(F32)


def _dot_by_mask(x, mask):
    hi = x.astype(BF16)
    r1 = x - hi.astype(F32)
    mid = r1.astype(BF16)
    lo = (r1 - mid.astype(F32)).astype(BF16)
    return (jnp.dot(hi, mask, preferred_element_type=F32) + jnp.dot(mid, mask, preferred_element_type=F32)
            + jnp.dot(lo, mask, preferred_element_type=F32))


def _transpose_exact(x):
    return lax.dot_general(_eye(x.shape[1]), x, _NT, precision=HIGHEST, preferred_element_type=F32)


def _mlstm_kernel(q_ref, k_ref, v_ref, o_ref, g_ref, bg_ref, ng_ref, c0_ref, n0_ref, m0_ref,
                  y_ref, c_ref, n_ref, m_ref, *, heads, dk):
    h = pl.program_id(1)

    @pl.when(pl.program_id(2) == 0)
    def _():
        c_ref[...] = c0_ref[...]
        n_ref[...] = n0_ref[...]
        m_ref[...] = m0_ref[...]

    length = q_ref.shape[0]
    gates = g_ref[...] + bg_ref[...]
    lane = lax.broadcasted_iota(jnp.int32, gates.shape, 1)
    ig_col = jnp.sum(jnp.where(lane == h, gates, 0.0), axis=1, keepdims=True)
    lf_col = jax.nn.log_sigmoid(jnp.sum(jnp.where(lane == h + heads, gates, 0.0), axis=1, keepdims=True))
    ig_b = jnp.broadcast_to(ig_col, (length, LANES))
    lf_b = jnp.broadcast_to(lf_col, (length, LANES))
    tri = _tril(length)
    b_b = jnp.dot(tri, lf_b, precision=HIGHEST, preferred_element_type=F32)
    b_col = b_b[:, 0:1]
    sel = (lax.broadcasted_iota(jnp.int32, (SUBLANES, LANES), 1) == 0).astype(F32)
    b_row = lax.dot_general(sel, b_b, _NT, precision=HIGHEST, preferred_element_type=F32)[0:1]
    ig_row = lax.dot_general(sel, ig_b, _NT, precision=HIGHEST, preferred_element_type=F32)[0:1]

    m_prev = m_ref[...]
    causal = tri > 0.5
    dmat = jnp.where(causal, b_col - b_row + ig_row, -jnp.inf)
    inter = b_col + m_prev
    m_t = jnp.maximum(inter, jnp.max(dmat, axis=1, keepdims=True))
    w = jnp.exp(dmat - m_t)
    a_inter = jnp.exp(inter - m_t)

    q = q_ref[...] * (dk ** -0.5)
    k = k_ref[...]
    v = v_ref[...]
    qb, kb, vb = q.astype(BF16), k.astype(BF16), v.astype(BF16)
    cmat = c_ref[...]
    nvec = n_ref[...]
    s = lax.dot_general(qb, kb, _NT, preferred_element_type=F32) * w
    num = a_inter * jnp.dot(qb, cmat.astype(BF16), preferred_element_type=F32) \
        + jnp.dot(s.astype(BF16), vb, preferred_element_type=F32)
    den = a_inter * jnp.sum(q * nvec, axis=1, keepdims=True) + jnp.sum(s, axis=1, keepdims=True)
    hh = num / jnp.maximum(jnp.abs(den), jnp.exp(-m_t))

    mu = jnp.mean(hh, axis=1, keepdims=True)
    hc = hh - mu
    var = jnp.mean(hc * hc, axis=1, keepdims=True)
    hn = hc * lax.rsqrt(var + LN_EPS) * ng_ref[...]
    y_ref[...] = jax.nn.sigmoid(o_ref[...]) * hn

    m_new = m_t[length - 1:length, :]
    carry = jnp.exp(inter[length - 1:length, :] - m_new)
    src = jnp.exp(b_col[length - 1:length, :] - b_col + ig_col - m_new)
    kt = k * src
    c_ref[...] = carry * cmat + lax.dot_general(kt.astype(BF16), vb, _TN, preferred_element_type=F32)
    n_ref[...] = carry * nvec + jnp.sum(kt, axis=0, keepdims=True)
    m_ref[...] = m_new


def _mlstm(proj, gates, b_gate, norm_g, c0, n0, m0, batch, seq):
    _, heads, dk, dv = c0.shape
    chunk = _pick(seq, 256) if seq % M_CHUNK_PREF == 0 else seq
    nc = seq // chunk
    kq = heads
    kv = 2 * heads * dk // dv
    ko = kv + heads

    def rows(b, h, c):
        return b * nc + c

    state_c = pl.BlockSpec((None, None, dk, dv), lambda b, h, c: (b, h, 0, 0))
    state_n = pl.BlockSpec((None, None, 1, dk), lambda b, h, c: (b, h, 0, 0))
    state_m = pl.BlockSpec((None, None, 1, 1), lambda b, h, c: (b, h, 0, 0))
    y, c1, n1, m1 = pl.pallas_call(
        functools.partial(_mlstm_kernel, heads=heads, dk=dk),
        grid=(batch, heads, nc),
        in_specs=[pl.BlockSpec((chunk, dk), lambda b, h, c: (rows(b, h, c), h)),
                  pl.BlockSpec((chunk, dk), lambda b, h, c: (rows(b, h, c), kq + h)),
                  pl.BlockSpec((chunk, dv), lambda b, h, c: (rows(b, h, c), kv + h)),
                  pl.BlockSpec((chunk, dv), lambda b, h, c: (rows(b, h, c), ko + h)),
                  pl.BlockSpec((chunk, LANES), lambda b, h, c: (rows(b, h, c), 0)),
                  pl.BlockSpec((1, LANES), lambda b, h, c: (0, 0)),
                  pl.BlockSpec((None, 1, dv), lambda b, h, c: (h, 0, 0)),
                  state_c, state_n, state_m],
        out_specs=[pl.BlockSpec((chunk, dv), lambda b, h, c: (rows(b, h, c), h)), state_c, state_n, state_m],
        out_shape=[jax.ShapeDtypeStruct((batch * seq, heads * dv), F32),
                   jax.ShapeDtypeStruct((batch, heads, dk, dv), F32),
                   jax.ShapeDtypeStruct((batch, heads, 1, dk), F32),
                   jax.ShapeDtypeStruct((batch, heads, 1, 1), F32)],
        compiler_params=_params(("parallel", "parallel", "arbitrary")),
        name="mlstm",
    )(proj, proj, proj, proj, gates, b_gate, norm_g.reshape(heads, 1, dv),
      c0, n0.reshape(batch, heads, 1, dk), m0.reshape(batch, heads, 1, 1))
    return y, c1, n1.reshape(batch, heads, dk), m1.reshape(batch, heads)


def _hgrn_kernel(q_ref, f_ref, i_ref, g_ref, lbw_ref, ng_ref, s0_ref, y_ref, s_ref, st_ref,
                 *, layer, chunk, hb):
    tb = pl.program_id(2)
    dh = LANES
    srow8 = lax.broadcasted_iota(jnp.int32, (SUBLANES, dh), 0)

    @pl.when(tb == 0)
    def _():
        for hh in range(hb):
            st_ref[hh] = s0_ref[hh].T

    lbw = lbw_ref[...]
    e = jnp.exp(lbw - jnp.max(lbw, axis=0, keepdims=True))
    soft = e / jnp.sum(e, axis=0, keepdims=True)
    lb_all = jnp.sum(soft[0:layer + 1], axis=0, keepdims=True) - soft[0:1]

    tri = _tril(chunk)
    n_chunks = q_ref.shape[0] // chunk

    def chunk_body(ci, carry):
        r0 = pl.multiple_of(ci * chunk, chunk)
        for hh in range(hb):
            cs = slice(hh * dh, (hh + 1) * dh)
            lb = lb_all[:, cs]
            fg = lb + (1.0 - lb) * jax.nn.sigmoid(f_ref[pl.ds(r0, chunk), cs])
            kk = 1.0 - fg
            logf = jnp.log(fg)
            qx = q_ref[pl.ds(r0, chunk), cs]
            qa = qx * jax.nn.sigmoid(qx)
            v = i_ref[pl.ds(r0, chunk), cs]
            bc = jnp.dot(tri, logf, precision=HIGHEST, preferred_element_type=F32)
            st = st_ref[hh]
            o = lax.dot_general((qa * jnp.exp(bc)).astype(BF16), st.astype(BF16), _NT, preferred_element_type=F32)
            rows_out = []
            for t in range(chunk):
                lo = (t // SUBLANES) * SUBLANES
                nb = lo + SUBLANES
                rel = jnp.where(srow8 <= t - lo, bc[t:t + 1, :] - bc[lo:nb, :], -jnp.inf)
                if lo:
                    rel = jnp.concatenate([bc[t:t + 1, :] - bc[0:lo, :], rel], axis=0)
                p = qa[t:t + 1, :] * kk[0:nb, :] * jnp.exp(rel)
                a = jnp.sum(p, axis=1, keepdims=True)
                rows_out.append(jnp.sum(a * v[0:nb, :], axis=0, keepdims=True))
            o = o + jnp.concatenate(rows_out, axis=0)
            b_end = bc[chunk - 1:chunk, :]
            kt = kk * jnp.exp(b_end - bc)
            st_ref[hh] = st * jnp.exp(b_end) + lax.dot_general(v.astype(BF16), kt.astype(BF16), _TN,
                                                               preferred_element_type=F32)
            on = o * lax.rsqrt(jnp.mean(o * o, axis=1, keepdims=True) + LN_EPS) * ng_ref[:, cs]
            gx = g_ref[pl.ds(r0, chunk), cs]
            y_ref[pl.ds(r0, chunk), cs] = on * (gx * jax.nn.sigmoid(gx))
        return carry

    lax.fori_loop(0, n_chunks, chunk_body, 0)

    @pl.when(tb == pl.num_programs(2) - 1)
    def _():
        for hh in range(hb):
            s_ref[hh] = st_ref[hh].T


def _hgrn(proj, lower_bounds, layer, norm_g, s0, batch, seq):
    _, heads, dk, dv = s0.shape
    assert dk == LANES and dv == LANES
    d = heads * dk
    chunk = H_CHUNK_PREF if seq % H_CHUNK_PREF == 0 else seq
    tb = _pick(seq, 256) if seq % H_CHUNK_PREF == 0 else seq
    hb = _pick(heads, 4) if tb >= 128 else _pick(heads, 8)
    ng = heads // hb
    nt = seq // tb

    def col(part):
        return pl.BlockSpec((tb, hb * dk), lambda b, h, t: (b * nt + t, part * ng + h))

    state = pl.BlockSpec((None, hb, dk, dv), lambda b, h, t: (b, h, 0, 0))
    depth = lower_bounds.shape[0]
    y, s1 = pl.pallas_call(
        functools.partial(_hgrn_kernel, layer=layer, chunk=chunk, hb=hb),
        grid=(batch, ng, nt),
        in_specs=[col(0), col(1), col(2), col(3),
                  pl.BlockSpec((depth, hb * dk), lambda b, h, t: (0, h)),
                  pl.BlockSpec((1, hb * dv), lambda b, h, t: (0, h)),
                  state],
        out_specs=[pl.BlockSpec((tb, hb * dv), lambda b, h, t: (b * nt + t, h)), state],
        out_shape=[jax.ShapeDtypeStruct((batch * seq, d), F32),
                   jax.ShapeDtypeStruct(s0.shape, F32)],
        scratch_shapes=[pltpu.VMEM((hb, dv, dk), F32)],
        compiler_params=_params(("parallel", "parallel", "arbitrary")),
        name="hgrn",
    )(proj, proj, proj, proj, lower_bounds, norm_g.reshape(1, d), s0)
    return y, s1


def _topk_rows(s, k, payload=None):
    rows = lax.broadcasted_iota(jnp.int32, s.shape, 0)
    vals, picks = [], []
    for _ in range(k):
        m = jnp.max(s, axis=0, keepdims=True)
        pos = jnp.min(jnp.where(s == m, rows, s.shape[0]), axis=0, keepdims=True)
        hit = rows == pos
        vals.append(m)
        picks.append(pos if payload is None else jnp.max(jnp.where(hit, payload, -1), axis=0, keepdims=True))
        s = jnp.where(hit, -jnp.inf, s)
    return jnp.concatenate(vals, axis=0), jnp.concatenate(picks, axis=0)


def _route_kernel(q_ref, keys_ref, idx_ref, gate_ref, *, n_keys):
    half = q_ref.shape[1] // 2
    for g in range(q_ref.shape[0] // LANES):
        cols = slice(g * LANES, (g + 1) * LANES)
        qb = q_ref[cols, :].astype(BF16)
        tops = []
        for p in range(2):
            st = lax.dot_general(keys_ref[p].astype(BF16), qb[:, p * half:(p + 1) * half], _NT,
                                 preferred_element_type=F32)
            tops.append(_topk_rows(st, P_TOPK))
        (s1, i1), (s2, i2) = tops
        cand = jnp.concatenate([s1[a:a + 1] + s2 for a in range(P_TOPK)], axis=0)
        cidx = jnp.concatenate([i1[a:a + 1] * n_keys + i2 for a in range(P_TOPK)], axis=0)
        best, experts = _topk_rows(cand, P_TOPK, payload=cidx)
        ex = jnp.exp(best - best[0:1])
        idx_ref[:, cols] = experts
        gate_ref[:, cols] = ex / jnp.sum(ex, axis=0, keepdims=True)


def _route(q, sub_keys, layer):
    m = q.shape[0]
    _, heads, _, n_keys, half = sub_keys.shape
    tb = _pick(m, 512)
    out = pl.BlockSpec((None, P_TOPK, tb), lambda i, h: (h, 0, i))
    return pl.pallas_call(
        functools.partial(_route_kernel, n_keys=n_keys),
        grid=(m // tb, heads),
        in_specs=[pl.BlockSpec((tb, 2 * half), lambda i, h: (i, h)),
                  pl.BlockSpec((None, None, 2, n_keys, half), lambda i, h: (layer, h, 0, 0, 0))],
        out_specs=[out, out],
        out_shape=[jax.ShapeDtypeStruct((heads, P_TOPK, m), jnp.int32),
                   jax.ShapeDtypeStruct((heads, P_TOPK, m), F32)],
        compiler_params=_params(("parallel", "parallel")),
        name="route",
    )(q, sub_keys)


EVAL_TOKENS = 8
EVAL_BURST = 1


def _pack_tables(u_tab, v_tab):
    layers, experts, d = u_tab.shape
    shape = (layers, experts, d // LANES, LANES)
    return jnp.stack([u_tab.astype(BF16).reshape(shape), v_tab.astype(BF16).reshape(shape)], axis=2)


def _eval_kernel(idx0_ref, idx1_ref, x_ref, gate_ref, uv_hbm, y_ref, buf, sem, *, layer, n_picks):
    i = pl.program_id(0)
    n_steps = pl.num_programs(0)
    tb = idx1_ref.shape[0]
    s_rows = x_ref.shape[1]
    group = LANES // s_rows
    n_tiles = n_picks // group

    def issue_token(idx_ref, row, t, dst):
        for k in range(n_picks):
            e = idx_ref[row, k]
            p = t * n_picks + k
            pltpu.make_async_copy(uv_hbm.at[layer, e], buf.at[dst, p], sem.at[dst]).start(priority=k % 2)

    def wait_slot(s):
        pltpu.make_async_copy(buf.at[s], buf.at[s], sem.at[s]).wait()

    @pl.when(i == 0)
    def _():
        for t in range(tb):
            issue_token(idx0_ref, t, t, 0)

    lane_s = lax.broadcasted_iota(jnp.int32, (s_rows, LANES), 1)
    row_s = lax.broadcasted_iota(jnp.int32, (s_rows, LANES), 0)
    diag = (lane_s % s_rows == row_s).astype(F32)
    la = lax.broadcasted_iota(jnp.int32, (LANES, LANES), 0)
    lb = lax.broadcasted_iota(jnp.int32, (LANES, LANES), 1)
    same_expert = (la // s_rows == lb // s_rows).astype(BF16)
    pick_lane = (lax.broadcasted_iota(jnp.int32, (n_picks, LANES), 0) % group
                 == lax.broadcasted_iota(jnp.int32, (n_picks, LANES), 1) // s_rows).astype(BF16)
    tile_of = (lax.broadcasted_iota(jnp.int32, (n_tiles, n_picks), 1) // group
               == lax.broadcasted_iota(jnp.int32, (n_tiles, n_picks), 0)).astype(F32)

    def evaluate(row, t, slot):
        r0 = t * n_picks
        u2 = buf[slot, pl.ds(r0, n_picks), 0].reshape(n_picks * s_rows, LANES)
        v2 = buf[slot, pl.ds(r0, n_picks), 1].reshape(n_picks * s_rows, LANES)
        zt = lax.dot_general(x_ref[row].astype(BF16), u2, _NT, preferred_element_type=F32)
        part = jnp.concatenate(
            [jnp.sum(zt[:, c * LANES:(c + 1) * LANES] * diag, axis=0, keepdims=True) for c in range(n_tiles)],
            axis=0)
        act = _dot_by_mask(part, same_expert)
        gate = _dot_by_mask(tile_of * gate_ref[row:row + 1, :], pick_lane)
        a = 0.5 * act * (1.0 + lax.erf(act * (2.0 ** -0.5))) * gate
        a_sel = jnp.concatenate([diag * a[c:c + 1, :] for c in range(n_tiles)], axis=1)
        y_ref[row] = jnp.dot(a_sel.astype(BF16), v2, preferred_element_type=F32)

    wait_slot(0)
    for g in range(0, tb, EVAL_BURST):
        for t in range(g, g + EVAL_BURST):
            issue_token(idx0_ref, tb + t, t, 1)
        for t in range(g, g + EVAL_BURST):
            evaluate(t, t, 0)
    wait_slot(1)
    for g in range(0, tb, EVAL_BURST):
        for t in range(g, g + EVAL_BURST):
            issue_token(idx1_ref, t, t, 0)
        for t in range(g, g + EVAL_BURST):
            evaluate(tb + t, t, 1)

    @pl.when(i == n_steps - 1)
    def _():
        wait_slot(0)


def _peer_eval(x, idx, gate, uv_pack, layer):
    m, d = x.shape
    n_picks = idx.shape[1]
    s_rows = d // LANES
    tb = EVAL_TOKENS
    n = m // (2 * tb)
    xs = pl.BlockSpec((2 * tb, s_rows, LANES), lambda i: (i, 0, 0))
    y = pl.pallas_call(
        functools.partial(_eval_kernel, layer=layer, n_picks=n_picks),
        grid=(n,),
        in_specs=[pl.BlockSpec((2 * tb, n_picks), lambda i: (i, 0), memory_space=pltpu.SMEM),
                  pl.BlockSpec((tb, n_picks), lambda i: (jnp.minimum(2 * i + 2, 2 * n - 1), 0),
                               memory_space=pltpu.SMEM),
                  xs,
                  pl.BlockSpec((2 * tb, n_picks), lambda i: (i, 0)),
                  pl.BlockSpec(memory_space=pl.ANY)],
        out_specs=xs,
        out_shape=jax.ShapeDtypeStruct((m, s_rows, LANES), F32),
        scratch_shapes=[pltpu.VMEM((2, tb * n_picks, 2, s_rows, LANES), BF16),
                        pltpu.SemaphoreType.DMA((2,))],
        compiler_params=_params(("arbitrary",)),
        name="peer_eval",
    )(idx, idx, x.reshape(m, s_rows, LANES), gate, uv_pack)
    return y.reshape(m, d)


def _peer(h, w_query, sub_keys, uv_pack, layer):
    m = h.shape[0]
    q = _matmul(h, w_query, layer)
    idx_t, gate_t = _route(q, sub_keys, layer)
    idx = idx_t.reshape(-1, m).T
    gate = gate_t.reshape(-1, m).T
    return _peer_eval(h, idx, gate, uv_pack, layer)


def _trunk(x, mod, seq_off, mc, mn, mm, hs, ln_g, ln_b, m_w_in, w_gate, b_gate, m_norm_g, m_w_out,
           h_w_in, h_lower_bounds, h_norm_g, h_w_out, p_w_query, p_sub_keys, uv_pack):
    batch, seq, d = x.shape
    depth = mod.shape[0]
    alpha = (2 * depth) ** 0.25
    rows = batch * seq
    out_c, out_n, out_m, out_s = [], [], [], []
    hin = _modulate(x, mod, 0, seq_off)
    for li in range(depth):
        j = li // 2
        hin2 = hin
        if li % 2 == 0:
            n_main = m_w_in.shape[2] - 2 * m_norm_g.shape[1]
            proj = _matmul(hin2, m_w_in, j, n_cols=n_main)
            gates = _matmul(hin2, w_gate, j)
            y, c1, n1, m1 = _mlstm(proj, gates, b_gate[j:j + 1], m_norm_g[j], mc[j], mn[j], mm[j], batch, seq)
            mix = _matmul(y, m_w_out, j)
            out_c.append(c1)
            out_n.append(n1)
            out_m.append(m1)
        else:
            proj = _matmul(hin2, h_w_in, j)
            y, s1 = _hgrn(proj, h_lower_bounds, li, h_norm_g[j], hs[j], batch, seq)
            mix = _matmul(y, h_w_out, j)
            out_s.append(s1)
        x, hin = _res_ln(x, mix.reshape(batch, seq, d), mod, ln_g, ln_b, li, 0, seq_off, alpha, (li, 4, 3))
        ff = _peer(hin, p_w_query, p_sub_keys, uv_pack, li)
        nxt = (li + 1, 1, 0) if li + 1 < depth else None
        x, hin = _res_ln(x, ff.reshape(batch, seq, d), mod, ln_g, ln_b, li, 1, seq_off, alpha, nxt)
    return x, jnp.stack(out_c), jnp.stack(out_n), jnp.stack(out_m), jnp.stack(out_s)


def kernel(x_prompt, x_sample, c_prompt, c_sample, state_mlstm_C, state_mlstm_n, state_mlstm_m, state_hgrn_S, ada_w, ada_b, ln_g, ln_b, m_w_in, m_b_gate, m_norm_g, m_w_out, h_w_in, h_lower_bounds, h_norm_g, h_w_out, p_w_query, p_sub_keys, p_u, p_v):
    depth, d, _ = ada_w.shape
    n_a, m_heads, m_dv = m_norm_g.shape
    n_b = h_norm_g.shape[0]
    batch = x_prompt.shape[0]
    dec_batch = x_sample.shape[0]

    n_seq = dec_batch + batch
    pad = -n_seq % SUBLANES
    c_all = jnp.concatenate([c_sample, c_prompt, jnp.zeros((pad, d), F32)], axis=0)
    mod = _ada(c_all, ada_w, ada_b).reshape(depth, n_seq + pad, 1, 6 * d)

    n_main = m_w_in.shape[2] - 2 * m_heads
    w_gate = jnp.pad(m_w_in[:, :, n_main:], ((0, 0), (0, 0), (0, LANES - 2 * m_heads)))
    b_gate = jnp.pad(m_b_gate, ((0, 0), (0, LANES - 2 * m_heads)))
    uv_pack = _pack_tables(p_u, p_v)

    m_dk = state_mlstm_C.shape[3]
    _, _, h_heads, h_dk, h_dv = state_hgrn_S.shape
    z_c = jnp.zeros((n_a, batch, m_heads, m_dk, m_dv), F32)
    z_n = jnp.zeros((n_a, batch, m_heads, m_dk), F32)
    z_m = jnp.zeros((n_a, batch, m_heads), F32)
    z_s = jnp.zeros((n_b, batch, h_heads, h_dk, h_dv), F32)

    shared = (ln_g, ln_b, m_w_in, w_gate, b_gate, m_norm_g, m_w_out, h_w_in, h_lower_bounds, h_norm_g, h_w_out,
              p_w_query, p_sub_keys, uv_pack)
    y_p, p_c, p_n, p_m, p_s = _trunk(x_prompt, mod, dec_batch, z_c, z_n, z_m, z_s, *shared)
    y_s, s_c, s_n, s_m, s_s = _trunk(x_sample, mod, 0, state_mlstm_C, state_mlstm_n, state_mlstm_m, state_hgrn_S,
                                     *shared)
    return (y_p, y_s, p_c, p_n, p_m, p_s, s_c, s_n, s_m, s_s)
```

```python
import functools

import jax
import jax.numpy as jnp
from jax import lax
from jax.experimental import pallas as pl
from jax.experimental.pallas import tpu as pltpu

F32 = jnp.float32
BF16 = jnp.bfloat16
HIGHEST = lax.Precision.HIGHEST

LN_EPS = 1e-5
P_TOPK = 16
M_CHUNK_PREF = 64
H_CHUNK_PREF = 32
LANES = 128
SUBLANES = 8
VMEM_LIMIT = 56 * 1024 * 1024

_NT = (((1,), (1,)), ((), ()))
_TN = (((0,), (0,)), ((), ()))


def _params(sem):
    return pltpu.CompilerParams(dimension_semantics=sem, vmem_limit_bytes=VMEM_LIMIT)


def _pick(n, pref):
    t = min(pref, n)
    while n % t:
        t //= 2
    return t


def _matmul_kernel(a_ref, w_ref, o_ref, acc_ref):
    k = pl.program_id(2)

    @pl.when(k == 0)
    def _():
        acc_ref[...] = jnp.zeros_like(acc_ref)

    acc_ref[...] += jnp.dot(a_ref[...].astype(BF16), w_ref[...].astype(BF16), preferred_element_type=F32)

    @pl.when(k == pl.num_programs(2) - 1)
    def _():
        o_ref[...] = acc_ref[...]


def _matmul(a, w, layer, n_cols=None):
    m, kdim = a.shape
    n = n_cols if n_cols is not None else w.shape[2]
    tm, tn, tk = _pick(m, 2048 if a.dtype == BF16 else 1024), _pick(n, 1024), _pick(kdim, 1024)
    return pl.pallas_call(
        _matmul_kernel,
        grid=(m // tm, n // tn, kdim // tk),
        in_specs=[pl.BlockSpec((tm, tk), lambda i, j, k: (i, k)),
                  pl.BlockSpec((None, tk, tn), lambda i, j, k: (layer, k, j))],
        out_specs=pl.BlockSpec((tm, tn), lambda i, j, k: (i, j)),
        out_shape=jax.ShapeDtypeStruct((m, n), F32),
        scratch_shapes=[pltpu.VMEM((tm, tn), F32)],
        compiler_params=_params(("parallel", "parallel", "arbitrary")),
        name="matmul",
    )(a, w)


def _ada_kernel(c_ref, w_ref, b_ref, o_ref, acc_ref):
    k = pl.program_id(2)

    @pl.when(k == 0)
    def _():
        acc_ref[...] = jnp.zeros_like(acc_ref)

    cs = jax.nn.silu(c_ref[...])
    acc_ref[...] += jnp.dot(cs.astype(BF16), w_ref[...].astype(BF16), preferred_element_type=F32)

    @pl.when(k == pl.num_programs(2) - 1)
    def _():
        o_ref[...] = acc_ref[...] + b_ref[...]


def _ada(c_all, ada_w, ada_b):
    depth, d, n = ada_w.shape
    rows = c_all.shape[0]
    tn, tk = _pick(n, 2048), _pick(d, 1024)
    return pl.pallas_call(
        _ada_kernel,
        grid=(depth, n // tn, d // tk),
        in_specs=[pl.BlockSpec((rows, tk), lambda l, j, k: (0, k)),
                  pl.BlockSpec((None, tk, tn), lambda l, j, k: (l, k, j)),
                  pl.BlockSpec((None, 1, tn), lambda l, j, k: (l, 0, j))],
        out_specs=pl.BlockSpec((None, rows, tn), lambda l, j, k: (l, 0, j)),
        out_shape=jax.ShapeDtypeStruct((depth, rows, n), F32),
        scratch_shapes=[pltpu.VMEM((rows, tn), F32)],
        compiler_params=_params(("parallel", "parallel", "arbitrary")),
        name="ada",
    )(c_all, ada_w, ada_b.reshape(depth, 1, n))


def _seq_blocks(x3):
    s, r, d = x3.shape
    if r >= 128:
        return 1, _pick(r, 128)
    return _pick(s, max(1, 128 // r)), r


def _mod_spec(layer, chunk, sb, seq_off, d):
    return pl.BlockSpec((None, sb, 1, d), lambda s, r: (layer, seq_off // sb + s, 0, chunk))


def _store_rows(o_ref, h):
    o_ref[...] = h.reshape(o_ref.shape).astype(o_ref.dtype)


def _rows_spec(sb, rb, r, d):
    return pl.BlockSpec((sb * rb, d), lambda i, j: (i * (r // rb) + j, 0))


def _modulate_kernel(x_ref, sc_ref, sh_ref, o_ref):
    _store_rows(o_ref, x_ref[...] * (1.0 + sc_ref[...]) + sh_ref[...])


def _modulate(x3, mod, layer, seq_off):
    s, r, d = x3.shape
    sb, rb = _seq_blocks(x3)
    xs = pl.BlockSpec((sb, rb, d), lambda i, j: (i, j, 0))
    return pl.pallas_call(
        _modulate_kernel,
        grid=(s // sb, r // rb),
        in_specs=[xs, _mod_spec(layer, 1, sb, seq_off, d), _mod_spec(layer, 0, sb, seq_off, d)],
        out_specs=_rows_spec(sb, rb, r, d),
        out_shape=jax.ShapeDtypeStruct((s * r, d), BF16),
        compiler_params=_params(("parallel", "parallel")),
        name="modulate",
    )(x3, mod, mod)


def _res_ln_kernel(x_ref, y_ref, g_ref, lg_ref, lb_ref, *rest, alpha, modulate):
    z = alpha * x_ref[...] + g_ref[...] * y_ref[...]
    mu = jnp.mean(z, axis=-1, keepdims=True)
    zc = z - mu
    var = jnp.mean(zc * zc, axis=-1, keepdims=True)
    xn = zc * lax.rsqrt(var + LN_EPS) * lg_ref[...] + lb_ref[...]
    if modulate:
        sc_ref, sh_ref, xo_ref, ho_ref = rest
        xo_ref[...] = xn
        _store_rows(ho_ref, xn * (1.0 + sc_ref[...]) + sh_ref[...])
    else:
        (xo_ref,) = rest
        xo_ref[...] = xn


def _res_ln(x3, y3, mod, ln_g, ln_b, layer, sub, seq_off, alpha, next_mod):
    s, r, d = x3.shape
    sb, rb = _seq_blocks(x3)
    xs = pl.BlockSpec((sb, rb, d), lambda i, j: (i, j, 0))
    ln_spec = pl.BlockSpec((None, 1, d), lambda i, j: (2 * layer + sub, 0, 0))
    lg = ln_g.reshape(-1, 1, d)
    lb = ln_b.reshape(-1, 1, d)
    in_specs = [xs, xs, _mod_spec(layer, 2 + 3 * sub, sb, seq_off, d), ln_spec, ln_spec]
    args = [x3, y3, mod, lg, lb]
    out_shape = [jax.ShapeDtypeStruct(x3.shape, F32)]
    out_specs = [xs]
    if next_mod is not None:
        nl, sc_chunk, sh_chunk = next_mod
        in_specs += [_mod_spec(nl, sc_chunk, sb, seq_off, d), _mod_spec(nl, sh_chunk, sb, seq_off, d)]
        args += [mod, mod]
        out_shape.append(jax.ShapeDtypeStruct((s * r, d), BF16))
        out_specs.append(_rows_spec(sb, rb, r, d))
    out = pl.pallas_call(
        functools.partial(_res_ln_kernel, alpha=alpha, modulate=next_mod is not None),
        grid=(s // sb, r // rb),
        in_specs=in_specs, out_specs=out_specs, out_shape=out_shape,
        compiler_params=_params(("parallel", "parallel")),
        name="res_ln",
    )(*args)
    return (out[0], out[1]) if next_mod is not None else (out[0], None)


def _mixer_dtype(chunk):
    return BF16 if chunk % (2 * SUBLANES) == 0 else F32


def _eye(n, dtype=F32):
    r = lax.broadcasted_iota(jnp.int32, (n, n), 0)
    c = lax.broadcasted_iota(jnp.int32, (n, n), 1)
    return (r == c).astype(dtype)


def _tril(n):
    r = lax.broadcasted_iota(jnp.int32, (n, n), 0)
    c = lax.broadcasted_iota(jnp.int32, (n, n), 1)
    return (r >= c).astype(F32)


def _dot_by_mask(x, mask):
    hi = x.astype(BF16)
    r1 = x - hi.astype(F32)
    mid = r1.astype(BF16)
    lo = (r1 - mid.astype(F32)).astype(BF16)
    return (jnp.dot(hi, mask, preferred_element_type=F32) + jnp.dot(mid, mask, preferred_element_type=F32)
            + jnp.dot(lo, mask, preferred_element_type=F32))


def _transpose_exact(x):
    return lax.dot_general(_eye(x.shape[1]), x, _NT, precision=HIGHEST, preferred_element_type=F32)


def _mlstm_kernel(q_ref, k_ref, v_ref, o_ref, g_ref, bg_ref, ng_ref, c0_ref, n0_ref, m0_ref,
                  y_ref, c_ref, n_ref, m_ref, *, heads, dk):
    h = pl.program_id(1)

    @pl.when(pl.program_id(2) == 0)
    def _():
        c_ref[...] = c0_ref[...]
        n_ref[...] = n0_ref[...]
        m_ref[...] = m0_ref[...]

    length = q_ref.shape[0]
    gates = g_ref[...] + bg_ref[...]
    lane = lax.broadcasted_iota(jnp.int32, gates.shape, 1)
    ig_col = jnp.sum(jnp.where(lane == h, gates, 0.0), axis=1, keepdims=True)
    lf_col = jax.nn.log_sigmoid(jnp.sum(jnp.where(lane == h + heads, gates, 0.0), axis=1, keepdims=True))
    ig_b = jnp.broadcast_to(ig_col, (length, LANES))
    lf_b = jnp.broadcast_to(lf_col, (length, LANES))
    tri = _tril(length)
    b_b = jnp.dot(tri, lf_b, precision=HIGHEST, preferred_element_type=F32)
    b_col = b_b[:, 0:1]
    sel = (lax.broadcasted_iota(jnp.int32, (SUBLANES, LANES), 1) == 0).astype(F32)
    b_row = lax.dot_general(sel, b_b, _NT, precision=HIGHEST, preferred_element_type=F32)[0:1]
    ig_row = lax.dot_general(sel, ig_b, _NT, precision=HIGHEST, preferred_element_type=F32)[0:1]

    m_prev = m_ref[...]
    causal = tri > 0.5
    dmat = jnp.where(causal, b_col - b_row + ig_row, -jnp.inf)
    inter = b_col + m_prev
    m_t = jnp.maximum(inter, jnp.max(dmat, axis=1, keepdims=True))
    w = jnp.exp(dmat - m_t)
    a_inter = jnp.exp(inter - m_t)

    q = q_ref[...] * (dk ** -0.5)
    k = k_ref[...]
    v = v_ref[...]
    qb, kb, vb = q.astype(BF16), k.astype(BF16), v.astype(BF16)
    cmat = c_ref[...]
    nvec = n_ref[...]
    s = lax.dot_general(qb, kb, _NT, preferred_element_type=F32) * w
    num = a_inter * jnp.dot(qb, cmat.astype(BF16), preferred_element_type=F32) \
        + jnp.dot(s.astype(BF16), vb, preferred_element_type=F32)
    den = a_inter * jnp.sum(q * nvec, axis=1, keepdims=True) + jnp.sum(s, axis=1, keepdims=True)
    hh = num / jnp.maximum(jnp.abs(den), jnp.exp(-m_t))

    mu = jnp.mean(hh, axis=1, keepdims=True)
    hc = hh - mu
    var = jnp.mean(hc * hc, axis=1, keepdims=True)
    hn = hc * lax.rsqrt(var + LN_EPS) * ng_ref[...]
    y_ref[...] = (jax.nn.sigmoid(o_ref[...]) * hn).astype(y_ref.dtype)

    m_new = m_t[length - 1:length, :]
    carry = jnp.exp(inter[length - 1:length, :] - m_new)
    src = jnp.exp(b_col[length - 1:length, :] - b_col + ig_col - m_new)
    kt = k * src
    c_ref[...] = carry * cmat + lax.dot_general(kt.astype(BF16), vb, _TN, preferred_element_type=F32)
    n_ref[...] = carry * nvec + jnp.sum(kt, axis=0, keepdims=True)
    m_ref[...] = m_new


def _mlstm(proj, gates, b_gate, norm_g, c0, n0, m0, batch, seq):
    _, heads, dk, dv = c0.shape
    chunk = _pick(seq, 256) if seq % M_CHUNK_PREF == 0 else seq
    nc = seq // chunk
    kq = heads
    kv = 2 * heads * dk // dv
    ko = kv + heads

    def rows(b, h, c):
        return b * nc + c

    state_c = pl.BlockSpec((None, None, dk, dv), lambda b, h, c: (b, h, 0, 0))
    state_n = pl.BlockSpec((None, None, 1, dk), lambda b, h, c: (b, h, 0, 0))
    state_m = pl.BlockSpec((None, None, 1, 1), lambda b, h, c: (b, h, 0, 0))
    y, c1, n1, m1 = pl.pallas_call(
        functools.partial(_mlstm_kernel, heads=heads, dk=dk),
        grid=(batch, heads, nc),
        in_specs=[pl.BlockSpec((chunk, dk), lambda b, h, c: (rows(b, h, c), h)),
                  pl.BlockSpec((chunk, dk), lambda b, h, c: (rows(b, h, c), kq + h)),
                  pl.BlockSpec((chunk, dv), lambda b, h, c: (rows(b, h, c), kv + h)),
                  pl.BlockSpec((chunk, dv), lambda b, h, c: (rows(b, h, c), ko + h)),
                  pl.BlockSpec((chunk, LANES), lambda b, h, c: (rows(b, h, c), 0)),
                  pl.BlockSpec((1, LANES), lambda b, h, c: (0, 0)),
                  pl.BlockSpec((None, 1, dv), lambda b, h, c: (h, 0, 0)),
                  state_c, state_n, state_m],
        out_specs=[pl.BlockSpec((chunk, dv), lambda b, h, c: (rows(b, h, c), h)), state_c, state_n, state_m],
        out_shape=[jax.ShapeDtypeStruct((batch * seq, heads * dv), _mixer_dtype(chunk)),
                   jax.ShapeDtypeStruct((batch, heads, dk, dv), F32),
                   jax.ShapeDtypeStruct((batch, heads, 1, dk), F32),
                   jax.ShapeDtypeStruct((batch, heads, 1, 1), F32)],
        compiler_params=_params(("parallel", "parallel", "arbitrary")),
        name="mlstm",
    )(proj, proj, proj, proj, gates, b_gate, norm_g.reshape(heads, 1, dv),
      c0, n0.reshape(batch, heads, 1, dk), m0.reshape(batch, heads, 1, 1))
    return y, c1, n1.reshape(batch, heads, dk), m1.reshape(batch, heads)


def _hgrn_kernel(q_ref, f_ref, i_ref, g_ref, lbw_ref, ng_ref, s0_ref, y_ref, s_ref, st_ref,
                 *, layer, chunk, hb):
    tb = pl.program_id(2)
    dh = LANES
    srow8 = lax.broadcasted_iota(jnp.int32, (SUBLANES, dh), 0)

    @pl.when(tb == 0)
    def _():
        for hh in range(hb):
            st_ref[hh] = s0_ref[hh].T

    lbw = lbw_ref[...]
    e = jnp.exp(lbw - jnp.max(lbw, axis=0, keepdims=True))
    soft = e / jnp.sum(e, axis=0, keepdims=True)
    lb_all = jnp.sum(soft[0:layer + 1], axis=0, keepdims=True) - soft[0:1]

    tri = _tril(chunk)
    n_chunks = q_ref.shape[0] // chunk

    def chunk_body(ci, carry):
        r0 = pl.multiple_of(ci * chunk, chunk)
        for hh in range(hb):
            cs = slice(hh * dh, (hh + 1) * dh)
            lb = lb_all[:, cs]
            fg = lb + (1.0 - lb) * jax.nn.sigmoid(f_ref[pl.ds(r0, chunk), cs])
            kk = 1.0 - fg
            logf = jnp.log(fg)
            qx = q_ref[pl.ds(r0, chunk), cs]
            qa = qx * jax.nn.sigmoid(qx)
            v = i_ref[pl.ds(r0, chunk), cs]
            bc = jnp.dot(tri, logf, precision=HIGHEST, preferred_element_type=F32)
            st = st_ref[hh]
            o = lax.dot_general((qa * jnp.exp(bc)).astype(BF16), st.astype(BF16), _NT, preferred_element_type=F32)
            rows_out = []
            for t in range(chunk):
                lo = (t // SUBLANES) * SUBLANES
                nb = lo + SUBLANES
                rel = jnp.where(srow8 <= t - lo, bc[t:t + 1, :] - bc[lo:nb, :], -jnp.inf)
                if lo:
                    rel = jnp.concatenate([bc[t:t + 1, :] - bc[0:lo, :], rel], axis=0)
                p = qa[t:t + 1, :] * kk[0:nb, :] * jnp.exp(rel)
                a = jnp.sum(p, axis=1, keepdims=True)
                rows_out.append(jnp.sum(a * v[0:nb, :], axis=0, keepdims=True))
            o = o + jnp.concatenate(rows_out, axis=0)
            b_end = bc[chunk - 1:chunk, :]
            kt = kk * jnp.exp(b_end - bc)
            st_ref[hh] = st * jnp.exp(b_end) + lax.dot_general(v.astype(BF16), kt.astype(BF16), _TN,
                                                               preferred_element_type=F32)
            on = o * lax.rsqrt(jnp.mean(o * o, axis=1, keepdims=True) + LN_EPS) * ng_ref[:, cs]
            gx = g_ref[pl.ds(r0, chunk), cs]
            y_ref[pl.ds(r0, chunk), cs] = (on * (gx * jax.nn.sigmoid(gx))).astype(y_ref.dtype)
        return carry

    lax.fori_loop(0, n_chunks, chunk_body, 0)

    @pl.when(tb == pl.num_programs(2) - 1)
    def _():
        for hh in range(hb):
            s_ref[hh] = st_ref[hh].T


def _hgrn(proj, lower_bounds, layer, norm_g, s0, batch, seq):
    _, heads, dk, dv = s0.shape
    assert dk == LANES and dv == LANES
    d = heads * dk
    chunk = H_CHUNK_PREF if seq % H_CHUNK_PREF == 0 else seq
    tb = _pick(seq, 256) if seq % H_CHUNK_PREF == 0 else seq
    hb = _pick(heads, 4) if tb >= 128 else _pick(heads, 8)
    ng = heads // hb
    nt = seq // tb

    def col(part):
        return pl.BlockSpec((tb, hb * dk), lambda b, h, t: (b * nt + t, part * ng + h))

    state = pl.BlockSpec((None, hb, dk, dv), lambda b, h, t: (b, h, 0, 0))
    depth = lower_bounds.shape[0]
    y, s1 = pl.pallas_call(
        functools.partial(_hgrn_kernel, layer=layer, chunk=chunk, hb=hb),
        grid=(batch, ng, nt),
        in_specs=[col(0), col(1), col(2), col(3),
                  pl.BlockSpec((depth, hb * dk), lambda b, h, t: (0, h)),
                  pl.BlockSpec((1, hb * dv), lambda b, h, t: (0, h)),
                  state],
        out_specs=[pl.BlockSpec((tb, hb * dv), lambda b, h, t: (b * nt + t, h)), state],
        out_shape=[jax.ShapeDtypeStruct((batch * seq, d), _mixer_dtype(chunk)),
                   jax.ShapeDtypeStruct(s0.shape, F32)],
        scratch_shapes=[pltpu.VMEM((hb, dv, dk), F32)],
        compiler_params=_params(("parallel", "parallel", "arbitrary")),
        name="hgrn",
    )(proj, proj, proj, proj, lower_bounds, norm_g.reshape(1, d), s0)
    return y, s1


def _topk_rows(s, k, payload=None):
    rows = lax.broadcasted_iota(jnp.int32, s.shape, 0)
    vals, picks = [], []
    for _ in range(k):
        m = jnp.max(s, axis=0, keepdims=True)
        pos = jnp.min(jnp.where(s == m, rows, s.shape[0]), axis=0, keepdims=True)
        hit = rows == pos
        vals.append(m)
        picks.append(pos if payload is None else jnp.max(jnp.where(hit, payload, -1), axis=0, keepdims=True))
        s = jnp.where(hit, -jnp.inf, s)
    return jnp.concatenate(vals, axis=0), jnp.concatenate(picks, axis=0)


def _route_kernel(q_ref, keys_ref, idx_ref, gate_ref, *, n_keys):
    half = q_ref.shape[1] // 2
    for g in range(q_ref.shape[0] // LANES):
        cols = slice(g * LANES, (g + 1) * LANES)
        qb = q_ref[cols, :].astype(BF16)
        tops = []
        for p in range(2):
            st = lax.dot_general(keys_ref[p].astype(BF16), qb[:, p * half:(p + 1) * half], _NT,
                                 preferred_element_type=F32)
            tops.append(_topk_rows(st, P_TOPK))
        (s1, i1), (s2, i2) = tops
        width = [P_TOPK // (a + 1) for a in range(P_TOPK)]
        pad = -sum(width) % SUBLANES
        cand = jnp.concatenate([s1[a:a + 1] + s2[0:width[a]] for a in range(P_TOPK)]
                               + [jnp.full((pad, LANES), -jnp.inf, F32)], axis=0)
        cidx = jnp.concatenate([i1[a:a + 1] * n_keys + i2[0:width[a]] for a in range(P_TOPK)]
                               + [jnp.full((pad, LANES), -1, jnp.int32)], axis=0)
        best, experts = _topk_rows(cand, P_TOPK, payload=cidx)
        ex = jnp.exp(best - best[0:1])
        idx_ref[:, cols] = experts
        gate_ref[:, cols] = ex / jnp.sum(ex, axis=0, keepdims=True)


def _route(q, sub_keys, layer):
    m = q.shape[0]
    _, heads, _, n_keys, half = sub_keys.shape
    tb = _pick(m, 512)
    out = pl.BlockSpec((None, P_TOPK, tb), lambda i, h: (h, 0, i))
    return pl.pallas_call(
        functools.partial(_route_kernel, n_keys=n_keys),
        grid=(m // tb, heads),
        in_specs=[pl.BlockSpec((tb, 2 * half), lambda i, h: (i, h)),
                  pl.BlockSpec((None, None, 2, n_keys, half), lambda i, h: (layer, h, 0, 0, 0))],
        out_specs=[out, out],
        out_shape=[jax.ShapeDtypeStruct((heads, P_TOPK, m), jnp.int32),
                   jax.ShapeDtypeStruct((heads, P_TOPK, m), F32)],
        compiler_params=_params(("parallel", "parallel")),
        name="route",
    )(q, sub_keys)


EVAL_TOKENS = 8


PACK_EXPERTS = 64


def _pack_kernel(u_ref, v_ref, o_ref):
    te, _, s_rows, _ = o_ref.shape
    o_ref[:, 0] = u_ref[...].reshape(te, s_rows, LANES).astype(BF16)
    o_ref[:, 1] = v_ref[...].reshape(te, s_rows, LANES).astype(BF16)


def _pack_tables(u_tab, v_tab):
    layers, experts, d = u_tab.shape
    te = _pick(experts, PACK_EXPERTS)
    s_rows = d // LANES
    src = pl.BlockSpec((None, te, d), lambda l, i: (l, i, 0))
    return pl.pallas_call(
        _pack_kernel,
        grid=(layers, experts // te),
        in_specs=[src, src],
        out_specs=pl.BlockSpec((None, te, 2, s_rows, LANES), lambda l, i: (l, i, 0, 0, 0)),
        out_shape=jax.ShapeDtypeStruct((layers, experts, 2, s_rows, LANES), BF16),
        compiler_params=_params(("parallel", "parallel")),
        name="pack_tables",
    )(u_tab, v_tab)


def _eval_kernel(idx0_ref, idx1_ref, x_ref, gate_ref, uv_hbm, y_ref, buf, sem, *, layer, n_picks):
    i = pl.program_id(0)
    n_steps = pl.num_programs(0)
    tb = idx1_ref.shape[0]
    s_rows = buf.shape[3]
    group = LANES // s_rows
    n_tiles = n_picks // group

    def issue(idx_ref, row, t, dst, k0, k1):
        for k in range(k0, k1):
            e = idx_ref[row, k]
            p = t * n_picks + k
            pltpu.make_async_copy(uv_hbm.at[layer, e], buf.at[dst, p], sem.at[dst]).start(priority=k % 2)

    def wait_slot(s):
        pltpu.make_async_copy(buf.at[s], buf.at[s], sem.at[s]).wait()

    @pl.when(i == 0)
    def _():
        for t in range(tb):
            issue(idx0_ref, t, t, 0, 0, n_picks)

    lane_s = lax.broadcasted_iota(jnp.int32, (s_rows, LANES), 1)
    row_s = lax.broadcasted_iota(jnp.int32, (s_rows, LANES), 0)
    diag = (lane_s % s_rows == row_s).astype(F32)
    sr = lax.broadcasted_iota(jnp.int32, (n_picks, n_picks * SUBLANES), 0)
    sk = lax.broadcasted_iota(jnp.int32, (n_picks, n_picks * SUBLANES), 1) // SUBLANES
    sub_sum = ((sr % n_tiles == sk // group) & (sr // n_tiles == sk % group)).astype(BF16)
    lane_sum = (lax.broadcasted_iota(jnp.int32, (group * LANES, LANES), 0) // LANES
                == lax.broadcasted_iota(jnp.int32, (group * LANES, LANES), 1) // s_rows).astype(BF16)
    pick_lane = (lax.broadcasted_iota(jnp.int32, (n_picks, LANES), 0) % group
                 == lax.broadcasted_iota(jnp.int32, (n_picks, LANES), 1) // s_rows).astype(BF16)
    tile_of = (lax.broadcasted_iota(jnp.int32, (n_tiles, n_picks), 1) // group
               == lax.broadcasted_iota(jnp.int32, (n_tiles, n_picks), 0)).astype(F32)

    def evaluate(row, t, slot, next_idx_ref, next_row):
        r0 = t * n_picks
        xf = jnp.concatenate([x_all[row:row + 1, s * LANES:(s + 1) * LANES] for s in range(s_rows)], axis=0)
        parts = []
        for c in range(n_tiles):
            issue(next_idx_ref, next_row, t, 1 - slot, c * group, (c + 1) * group)
            prod = buf[slot, pl.ds(r0 + c * group, group), 0].astype(F32) * xf[None]
            parts.append(prod.reshape(group, s_rows // SUBLANES, SUBLANES, LANES).sum(axis=1)
                         .reshape(group * SUBLANES, LANES))
        pm = jnp.concatenate(parts, axis=0)
        hi = pm.astype(BF16)
        lo = (pm - hi.astype(F32)).astype(BF16)
        s1 = jnp.dot(sub_sum, hi, preferred_element_type=F32) + jnp.dot(sub_sum, lo, preferred_element_type=F32)
        s1 = jnp.concatenate([s1[g * n_tiles:(g + 1) * n_tiles] for g in range(group)], axis=1)
        act = _dot_by_mask(s1, lane_sum)
        gate = _dot_by_mask(tile_of * gate_ref[row:row + 1, :], pick_lane)
        a = 0.5 * act * (1.0 + lax.erf(act * (2.0 ** -0.5))) * gate
        a_sel = jnp.concatenate([diag * a[c:c + 1, :] for c in range(n_tiles)], axis=1)
        v2 = buf[slot, pl.ds(r0, n_picks), 1].reshape(n_picks * s_rows, LANES)
        y2 = jnp.dot(a_sel.astype(BF16), v2, preferred_element_type=F32)
        for s in range(s_rows):
            y_ref[row:row + 1, s * LANES:(s + 1) * LANES] = y2[s:s + 1, :]

    x_all = x_ref[...].astype(F32)

    wait_slot(0)
    for t in range(tb):
        evaluate(t, t, 0, idx0_ref, tb + t)
    wait_slot(1)
    for t in range(tb):
        evaluate(tb + t, t, 1, idx1_ref, t)

    @pl.when(i == n_steps - 1)
    def _():
        wait_slot(0)


def _peer_eval(x, idx, gate, uv_pack, layer):
    m, d = x.shape
    n_picks = idx.shape[1]
    s_rows = d // LANES
    tb = EVAL_TOKENS
    n = m // (2 * tb)
    xs = pl.BlockSpec((2 * tb, d), lambda i: (i, 0))
    return pl.pallas_call(
        functools.partial(_eval_kernel, layer=layer, n_picks=n_picks),
        grid=(n,),
        in_specs=[pl.BlockSpec((2 * tb, n_picks), lambda i: (i, 0), memory_space=pltpu.SMEM),
                  pl.BlockSpec((tb, n_picks), lambda i: (jnp.minimum(2 * i + 2, 2 * n - 1), 0),
                               memory_space=pltpu.SMEM),
                  xs,
                  pl.BlockSpec((2 * tb, n_picks), lambda i: (i, 0)),
                  pl.BlockSpec(memory_space=pl.ANY)],
        out_specs=xs,
        out_shape=jax.ShapeDtypeStruct((m, d), F32),
        scratch_shapes=[pltpu.VMEM((2, tb * n_picks, 2, s_rows, LANES), BF16),
                        pltpu.SemaphoreType.DMA((2,))],
        compiler_params=_params(("arbitrary",)),
        name="peer_eval",
    )(idx, idx, x, gate, uv_pack)


def _peer(h, w_query, sub_keys, uv_pack, layer):
    m = h.shape[0]
    q = _matmul(h, w_query, layer)
    idx_t, gate_t = _route(q, sub_keys, layer)
    idx = idx_t.reshape(-1, m).T
    gate = gate_t.reshape(-1, m).T
    return _peer_eval(h, idx, gate, uv_pack, layer)


def _trunk(x, mod, seq_off, mc, mn, mm, hs, ln_g, ln_b, m_w_in, w_gate, b_gate, m_norm_g, m_w_out,
           h_w_in, h_lower_bounds, h_norm_g, h_w_out, p_w_query, p_sub_keys, uv_pack):
    batch, seq, d = x.shape
    depth = mod.shape[0]
    alpha = (2 * depth) ** 0.25
    rows = batch * seq
    out_c, out_n, out_m, out_s = [], [], [], []
    hin = _modulate(x, mod, 0, seq_off)
    for li in range(depth):
        j = li // 2
        hin2 = hin
        if li % 2 == 0:
            n_main = m_w_in.shape[2] - 2 * m_norm_g.shape[1]
            proj = _matmul(hin2, m_w_in, j, n_cols=n_main)
            gates = _matmul(hin2, w_gate, j)
            y, c1, n1, m1 = _mlstm(proj, gates, b_gate[j:j + 1], m_norm_g[j], mc[j], mn[j], mm[j], batch, seq)
            mix = _matmul(y, m_w_out, j)
            out_c.append(c1)
            out_n.append(n1)
            out_m.append(m1)
        else:
            proj = _matmul(hin2, h_w_in, j)
            y, s1 = _hgrn(proj, h_lower_bounds, li, h_norm_g[j], hs[j], batch, seq)
            mix = _matmul(y, h_w_out, j)
            out_s.append(s1)
        x, hin = _res_ln(x, mix.reshape(batch, seq, d), mod, ln_g, ln_b, li, 0, seq_off, alpha, (li, 4, 3))
        ff = _peer(hin, p_w_query, p_sub_keys, uv_pack, li)
        nxt = (li + 1, 1, 0) if li + 1 < depth else None
        x, hin = _res_ln(x, ff.reshape(batch, seq, d), mod, ln_g, ln_b, li, 1, seq_off, alpha, nxt)
    return x, jnp.stack(out_c), jnp.stack(out_n), jnp.stack(out_m), jnp.stack(out_s)


def kernel(x_prompt, x_sample, c_prompt, c_sample, state_mlstm_C, state_mlstm_n, state_mlstm_m, state_hgrn_S, ada_w, ada_b, ln_g, ln_b, m_w_in, m_b_gate, m_norm_g, m_w_out, h_w_in, h_lower_bounds, h_norm_g, h_w_out, p_w_query, p_sub_keys, p_u, p_v):
    depth, d, _ = ada_w.shape
    n_a, m_heads, m_dv = m_norm_g.shape
    n_b = h_norm_g.shape[0]
    batch = x_prompt.shape[0]
    dec_batch = x_sample.shape[0]

    n_seq = dec_batch + batch
    pad = -n_seq % SUBLANES
    c_all = jnp.concatenate([c_sample, c_prompt, jnp.zeros((pad, d), F32)], axis=0)
    mod = _ada(c_all, ada_w, ada_b).reshape(depth, n_seq + pad, 1, 6 * d)

    n_main = m_w_in.shape[2] - 2 * m_heads
    w_gate = jnp.pad(m_w_in[:, :, n_main:], ((0, 0), (0, 0), (0, LANES - 2 * m_heads)))
    b_gate = jnp.pad(m_b_gate, ((0, 0), (0, LANES - 2 * m_heads)))
    uv_pack = _pack_tables(p_u, p_v)

    m_dk = state_mlstm_C.shape[3]
    _, _, h_heads, h_dk, h_dv = state_hgrn_S.shape
    z_c = jnp.zeros((n_a, batch, m_heads, m_dk, m_dv), F32)
    z_n = jnp.zeros((n_a, batch, m_heads, m_dk), F32)
    z_m = jnp.zeros((n_a, batch, m_heads), F32)
    z_s = jnp.zeros((n_b, batch, h_heads, h_dk, h_dv), F32)

    shared = (ln_g, ln_b, m_w_in, w_gate, b_gate, m_norm_g, m_w_out, h_w_in, h_lower_bounds, h_norm_g, h_w_out,
              p_w_query, p_sub_keys, uv_pack)
    y_p, p_c, p_n, p_m, p_s = _trunk(x_prompt, mod, dec_batch, z_c, z_n, z_m, z_s, *shared)
    y_s, s_c, s_n, s_m, s_s = _trunk(x_sample, mod, 0, state_mlstm_C, state_mlstm_n, state_mlstm_m, state_hgrn_S,
                                     *shared)
    return (y_p, y_s, p_c, p_n, p_m, p_s, s_c, s_n, s_m, s_s)
```

```python
import functools

import jax
import jax.numpy as jnp
from jax import lax
from jax.experimental import pallas as pl
from jax.experimental.pallas import tpu as pltpu

F32 = jnp.float32
BF16 = jnp.bfloat16
HIGHEST = lax.Precision.HIGHEST

LN_EPS = 1e-5
P_TOPK = 16
M_CHUNK_PREF = 64
H_CHUNK_PREF = 32
LANES = 128
SUBLANES = 8
VMEM_LIMIT = 56 * 1024 * 1024

_NT = (((1,), (1,)), ((), ()))
_TN = (((0,), (0,)), ((), ()))


def _params(sem):
    return pltpu.CompilerParams(dimension_semantics=sem, vmem_limit_bytes=VMEM_LIMIT)


def _pick(n, pref):
    t = min(pref, n)
    while n % t:
        t //= 2
    return t


def _matmul_kernel(a_ref, w_ref, o_ref, acc_ref):
    k = pl.program_id(2)

    @pl.when(k == 0)
    def _():
        acc_ref[...] = jnp.zeros_like(acc_ref)

    acc_ref[...] += jnp.dot(a_ref[...].astype(BF16), w_ref[...].astype(BF16), preferred_element_type=F32)

    @pl.when(k == pl.num_programs(2) - 1)
    def _():
        o_ref[...] = acc_ref[...]


def _matmul(a, w, layer, n_cols=None):
    m, kdim = a.shape
    n = n_cols if n_cols is not None else w.shape[2]
    tm, tn, tk = _pick(m, 2048 if a.dtype == BF16 else 1024), _pick(n, 1024), _pick(kdim, 1024)
    return pl.pallas_call(
        _matmul_kernel,
        grid=(m // tm, n // tn, kdim // tk),
        in_specs=[pl.BlockSpec((tm, tk), lambda i, j, k: (i, k)),
                  pl.BlockSpec((None, tk, tn), lambda i, j, k: (layer, k, j))],
        out_specs=pl.BlockSpec((tm, tn), lambda i, j, k: (i, j)),
        out_shape=jax.ShapeDtypeStruct((m, n), F32),
        scratch_shapes=[pltpu.VMEM((tm, tn), F32)],
        compiler_params=_params(("parallel", "parallel", "arbitrary")),
        name="matmul",
    )(a, w)


def _ada_kernel(c_ref, w_ref, b_ref, o_ref, acc_ref):
    k = pl.program_id(2)

    @pl.when(k == 0)
    def _():
        acc_ref[...] = jnp.zeros_like(acc_ref)

    cs = jax.nn.silu(c_ref[...])
    acc_ref[...] += jnp.dot(cs.astype(BF16), w_ref[...].astype(BF16), preferred_element_type=F32)

    @pl.when(k == pl.num_programs(2) - 1)
    def _():
        o_ref[...] = acc_ref[...] + b_ref[...]


def _ada(c_all, ada_w, ada_b):
    depth, d, n = ada_w.shape
    rows = c_all.shape[0]
    tn, tk = _pick(n, 2048), _pick(d, 1024)
    return pl.pallas_call(
        _ada_kernel,
        grid=(depth, n // tn, d // tk),
        in_specs=[pl.BlockSpec((rows, tk), lambda l, j, k: (0, k)),
                  pl.BlockSpec((None, tk, tn), lambda l, j, k: (l, k, j)),
                  pl.BlockSpec((None, 1, tn), lambda l, j, k: (l, 0, j))],
        out_specs=pl.BlockSpec((None, rows, tn), lambda l, j, k: (l, 0, j)),
        out_shape=jax.ShapeDtypeStruct((depth, rows, n), F32),
        scratch_shapes=[pltpu.VMEM((rows, tn), F32)],
        compiler_params=_params(("parallel", "parallel", "arbitrary")),
        name="ada",
    )(c_all, ada_w, ada_b.reshape(depth, 1, n))


def _seq_blocks(x3):
    s, r, d = x3.shape
    if r >= 128:
        return 1, _pick(r, 128)
    return _pick(s, max(1, 128 // r)), r


def _mod_spec(layer, chunk, sb, seq_off, d):
    return pl.BlockSpec((None, sb, 1, d), lambda s, r: (layer, seq_off // sb + s, 0, chunk))


def _store_rows(o_ref, h):
    o_ref[...] = h.reshape(o_ref.shape).astype(o_ref.dtype)


def _rows_spec(sb, rb, r, d):
    return pl.BlockSpec((sb * rb, d), lambda i, j: (i * (r // rb) + j, 0))


def _modulate_kernel(x_ref, sc_ref, sh_ref, o_ref):
    _store_rows(o_ref, x_ref[...] * (1.0 + sc_ref[...]) + sh_ref[...])


def _modulate(x3, mod, layer, seq_off):
    s, r, d = x3.shape
    sb, rb = _seq_blocks(x3)
    xs = pl.BlockSpec((sb, rb, d), lambda i, j: (i, j, 0))
    return pl.pallas_call(
        _modulate_kernel,
        grid=(s // sb, r // rb),
        in_specs=[xs, _mod_spec(layer, 1, sb, seq_off, d), _mod_spec(layer, 0, sb, seq_off, d)],
        out_specs=_rows_spec(sb, rb, r, d),
        out_shape=jax.ShapeDtypeStruct((s * r, d), BF16),
        compiler_params=_params(("parallel", "parallel")),
        name="modulate",
    )(x3, mod, mod)


def _res_ln_kernel(x_ref, y_ref, g_ref, lg_ref, lb_ref, *rest, alpha, modulate):
    z = alpha * x_ref[...] + g_ref[...] * y_ref[...]
    mu = jnp.mean(z, axis=-1, keepdims=True)
    zc = z - mu
    var = jnp.mean(zc * zc, axis=-1, keepdims=True)
    xn = zc * lax.rsqrt(var + LN_EPS) * lg_ref[...] + lb_ref[...]
    if modulate:
        sc_ref, sh_ref, xo_ref, ho_ref = rest
        xo_ref[...] = xn
        _store_rows(ho_ref, xn * (1.0 + sc_ref[...]) + sh_ref[...])
    else:
        (xo_ref,) = rest
        xo_ref[...] = xn


def _res_ln(x3, y3, mod, ln_g, ln_b, layer, sub, seq_off, alpha, next_mod):
    s, r, d = x3.shape
    sb, rb = _seq_blocks(x3)
    xs = pl.BlockSpec((sb, rb, d), lambda i, j: (i, j, 0))
    ln_spec = pl.BlockSpec((None, 1, d), lambda i, j: (2 * layer + sub, 0, 0))
    lg = ln_g.reshape(-1, 1, d)
    lb = ln_b.reshape(-1, 1, d)
    in_specs = [xs, xs, _mod_spec(layer, 2 + 3 * sub, sb, seq_off, d), ln_spec, ln_spec]
    args = [x3, y3, mod, lg, lb]
    out_shape = [jax.ShapeDtypeStruct(x3.shape, F32)]
    out_specs = [xs]
    if next_mod is not None:
        nl, sc_chunk, sh_chunk = next_mod
        in_specs += [_mod_spec(nl, sc_chunk, sb, seq_off, d), _mod_spec(nl, sh_chunk, sb, seq_off, d)]
        args += [mod, mod]
        out_shape.append(jax.ShapeDtypeStruct((s * r, d), BF16))
        out_specs.append(_rows_spec(sb, rb, r, d))
    out = pl.pallas_call(
        functools.partial(_res_ln_kernel, alpha=alpha, modulate=next_mod is not None),
        grid=(s // sb, r // rb),
        in_specs=in_specs, out_specs=out_specs, out_shape=out_shape,
        compiler_params=_params(("parallel", "parallel")),
        name="res_ln",
    )(*args)
    return (out[0], out[1]) if next_mod is not None else (out[0], None)


def _mixer_dtype(chunk):
    return BF16 if chunk % (2 * SUBLANES) == 0 else F32


def _eye(n, dtype=F32):
    r = lax.broadcasted_iota(jnp.int32, (n, n), 0)
    c = lax.broadcasted_iota(jnp.int32, (n, n), 1)
    return (r == c).astype(dtype)


def _tril(n):
    r = lax.broadcasted_iota(jnp.int32, (n, n), 0)
    c = lax.broadcasted_iota(jnp.int32, (n, n), 1)
    return (r >= c).astype(F32)


def _dot_by_mask(x, mask):
    hi = x.astype(BF16)
    r1 = x - hi.astype(F32)
    mid = r1.astype(BF16)
    lo = (r1 - mid.astype(F32)).astype(BF16)
    return (jnp.dot(hi, mask, preferred_element_type=F32) + jnp.dot(mid, mask, preferred_element_type=F32)
            + jnp.dot(lo, mask, preferred_element_type=F32))


def _transpose_exact(x):
    return lax.dot_general(_eye(x.shape[1]), x, _NT, precision=HIGHEST, preferred_element_type=F32)


def _mlstm_kernel(q_ref, k_ref, v_ref, o_ref, g_ref, bg_ref, ng_ref, c0_ref, n0_ref, m0_ref,
                  y_ref, c_ref, n_ref, m_ref, *, heads, dk):
    h = pl.program_id(1)

    @pl.when(pl.program_id(2) == 0)
    def _():
        c_ref[...] = c0_ref[...]
        n_ref[...] = n0_ref[...]
        m_ref[...] = m0_ref[...]

    length = q_ref.shape[0]
    gates = g_ref[...] + bg_ref[...]
    lane = lax.broadcasted_iota(jnp.int32, gates.shape, 1)
    ig_col = jnp.sum(jnp.where(lane == h, gates, 0.0), axis=1, keepdims=True)
    lf_col = jax.nn.log_sigmoid(jnp.sum(jnp.where(lane == h + heads, gates, 0.0), axis=1, keepdims=True))
    ig_b = jnp.broadcast_to(ig_col, (length, LANES))
    lf_b = jnp.broadcast_to(lf_col, (length, LANES))
    tri = _tril(length)
    b_b = jnp.dot(tri, lf_b, precision=HIGHEST, preferred_element_type=F32)
    b_col = b_b[:, 0:1]
    sel = (lax.broadcasted_iota(jnp.int32, (SUBLANES, LANES), 1) == 0).astype(F32)
    b_row = lax.dot_general(sel, b_b, _NT, precision=HIGHEST, preferred_element_type=F32)[0:1]
    ig_row = lax.dot_general(sel, ig_b, _NT, precision=HIGHEST, preferred_element_type=F32)[0:1]

    m_prev = m_ref[...]
    causal = tri > 0.5
    dmat = jnp.where(causal, b_col - b_row + ig_row, -jnp.inf)
    inter = b_col + m_prev
    m_t = jnp.maximum(inter, jnp.max(dmat, axis=1, keepdims=True))
    w = jnp.exp(dmat - m_t)
    a_inter = jnp.exp(inter - m_t)

    q = q_ref[...] * (dk ** -0.5)
    k = k_ref[...]
    v = v_ref[...]
    qb, kb, vb = q.astype(BF16), k.astype(BF16), v.astype(BF16)
    cmat = c_ref[...]
    nvec = n_ref[...]
    s = lax.dot_general(qb, kb, _NT, preferred_element_type=F32) * w
    num = a_inter * jnp.dot(qb, cmat.astype(BF16), preferred_element_type=F32) \
        + jnp.dot(s.astype(BF16), vb, preferred_element_type=F32)
    den = a_inter * jnp.sum(q * nvec, axis=1, keepdims=True) + jnp.sum(s, axis=1, keepdims=True)
    hh = num / jnp.maximum(jnp.abs(den), jnp.exp(-m_t))

    mu = jnp.mean(hh, axis=1, keepdims=True)
    hc = hh - mu
    var = jnp.mean(hc * hc, axis=1, keepdims=True)
    hn = hc * lax.rsqrt(var + LN_EPS) * ng_ref[...]
    y_ref[...] = (jax.nn.sigmoid(o_ref[...]) * hn).astype(y_ref.dtype)

    m_new = m_t[length - 1:length, :]
    carry = jnp.exp(inter[length - 1:length, :] - m_new)
    src = jnp.exp(b_col[length - 1:length, :] - b_col + ig_col - m_new)
    kt = k * src
    c_ref[...] = carry * cmat + lax.dot_general(kt.astype(BF16), vb, _TN, preferred_element_type=F32)
    n_ref[...] = carry * nvec + jnp.sum(kt, axis=0, keepdims=True)
    m_ref[...] = m_new


def _mlstm(proj, gates, b_gate, norm_g, c0, n0, m0, batch, seq):
    _, heads, dk, dv = c0.shape
    chunk = _pick(seq, 256) if seq % M_CHUNK_PREF == 0 else seq
    nc = seq // chunk
    kq = heads
    kv = 2 * heads * dk // dv
    ko = kv + heads

    def rows(b, h, c):
        return b * nc + c

    state_c = pl.BlockSpec((None, None, dk, dv), lambda b, h, c: (b, h, 0, 0))
    state_n = pl.BlockSpec((None, None, 1, dk), lambda b, h, c: (b, h, 0, 0))
    state_m = pl.BlockSpec((None, None, 1, 1), lambda b, h, c: (b, h, 0, 0))
    y, c1, n1, m1 = pl.pallas_call(
        functools.partial(_mlstm_kernel, heads=heads, dk=dk),
        grid=(batch, heads, nc),
        in_specs=[pl.BlockSpec((chunk, dk), lambda b, h, c: (rows(b, h, c), h)),
                  pl.BlockSpec((chunk, dk), lambda b, h, c: (rows(b, h, c), kq + h)),
                  pl.BlockSpec((chunk, dv), lambda b, h, c: (rows(b, h, c), kv + h)),
                  pl.BlockSpec((chunk, dv), lambda b, h, c: (rows(b, h, c), ko + h)),
                  pl.BlockSpec((chunk, LANES), lambda b, h, c: (rows(b, h, c), 0)),
                  pl.BlockSpec((1, LANES), lambda b, h, c: (0, 0)),
                  pl.BlockSpec((None, 1, dv), lambda b, h, c: (h, 0, 0)),
                  state_c, state_n, state_m],
        out_specs=[pl.BlockSpec((chunk, dv), lambda b, h, c: (rows(b, h, c), h)), state_c, state_n, state_m],
        out_shape=[jax.ShapeDtypeStruct((batch * seq, heads * dv), _mixer_dtype(chunk)),
                   jax.ShapeDtypeStruct((batch, heads, dk, dv), F32),
                   jax.ShapeDtypeStruct((batch, heads, 1, dk), F32),
                   jax.ShapeDtypeStruct((batch, heads, 1, 1), F32)],
        compiler_params=_params(("parallel", "parallel", "arbitrary")),
        name="mlstm",
    )(proj, proj, proj, proj, gates, b_gate, norm_g.reshape(heads, 1, dv),
      c0, n0.reshape(batch, heads, 1, dk), m0.reshape(batch, heads, 1, 1))
    return y, c1, n1.reshape(batch, heads, dk), m1.reshape(batch, heads)


def _hgrn_kernel(q_ref, f_ref, i_ref, g_ref, lbw_ref, ng_ref, s0_ref, y_ref, s_ref, st_ref,
                 *, layer, chunk, hb):
    tb = pl.program_id(2)
    dh = LANES
    srow8 = lax.broadcasted_iota(jnp.int32, (SUBLANES, dh), 0)

    @pl.when(tb == 0)
    def _():
        for hh in range(hb):
            st_ref[hh] = s0_ref[hh].T

    lbw = lbw_ref[...]
    e = jnp.exp(lbw - jnp.max(lbw, axis=0, keepdims=True))
    soft = e / jnp.sum(e, axis=0, keepdims=True)
    lb_all = jnp.sum(soft[0:layer + 1], axis=0, keepdims=True) - soft[0:1]

    tri = _tril(chunk)
    n_chunks = q_ref.shape[0] // chunk

    def chunk_body(ci, carry):
        r0 = pl.multiple_of(ci * chunk, chunk)
        for hh in range(hb):
            cs = slice(hh * dh, (hh + 1) * dh)
            lb = lb_all[:, cs]
            fg = lb + (1.0 - lb) * jax.nn.sigmoid(f_ref[pl.ds(r0, chunk), cs])
            kk = 1.0 - fg
            logf = jnp.log(fg)
            qx = q_ref[pl.ds(r0, chunk), cs]
            qa = qx * jax.nn.sigmoid(qx)
            v = i_ref[pl.ds(r0, chunk), cs]
            bc = jnp.dot(tri, logf, precision=HIGHEST, preferred_element_type=F32)
            st = st_ref[hh]
            o = lax.dot_general((qa * jnp.exp(bc)).astype(BF16), st.astype(BF16), _NT, preferred_element_type=F32)
            rows_out = []
            for t in range(chunk):
                lo = (t // SUBLANES) * SUBLANES
                nb = lo + SUBLANES
                rel = jnp.where(srow8 <= t - lo, bc[t:t + 1, :] - bc[lo:nb, :], -jnp.inf)
                if lo:
                    rel = jnp.concatenate([bc[t:t + 1, :] - bc[0:lo, :], rel], axis=0)
                p = qa[t:t + 1, :] * kk[0:nb, :] * jnp.exp(rel)
                a = jnp.sum(p, axis=1, keepdims=True)
                rows_out.append(jnp.sum(a * v[0:nb, :], axis=0, keepdims=True))
            o = o + jnp.concatenate(rows_out, axis=0)
            b_end = bc[chunk - 1:chunk, :]
            kt = kk * jnp.exp(b_end - bc)
            st_ref[hh] = st * jnp.exp(b_end) + lax.dot_general(v.astype(BF16), kt.astype(BF16), _TN,
                                                               preferred_element_type=F32)
            on = o * lax.rsqrt(jnp.mean(o * o, axis=1, keepdims=True) + LN_EPS) * ng_ref[:, cs]
            gx = g_ref[pl.ds(r0, chunk), cs]
            y_ref[pl.ds(r0, chunk), cs] = (on * (gx * jax.nn.sigmoid(gx))).astype(y_ref.dtype)
        return carry

    lax.fori_loop(0, n_chunks, chunk_body, 0)

    @pl.when(tb == pl.num_programs(2) - 1)
    def _():
        for hh in range(hb):
            s_ref[hh] = st_ref[hh].T


def _hgrn(proj, lower_bounds, layer, norm_g, s0, batch, seq):
    _, heads, dk, dv = s0.shape
    assert dk == LANES and dv == LANES
    d = heads * dk
    chunk = H_CHUNK_PREF if seq % H_CHUNK_PREF == 0 else seq
    tb = _pick(seq, 256) if seq % H_CHUNK_PREF == 0 else seq
    hb = _pick(heads, 8)
    ng = heads // hb
    nt = seq // tb

    def col(part):
        return pl.BlockSpec((tb, hb * dk), lambda b, h, t: (b * nt + t, part * ng + h))

    state = pl.BlockSpec((None, hb, dk, dv), lambda b, h, t: (b, h, 0, 0))
    depth = lower_bounds.shape[0]
    y, s1 = pl.pallas_call(
        functools.partial(_hgrn_kernel, layer=layer, chunk=chunk, hb=hb),
        grid=(batch, ng, nt),
        in_specs=[col(0), col(1), col(2), col(3),
                  pl.BlockSpec((depth, hb * dk), lambda b, h, t: (0, h)),
                  pl.BlockSpec((1, hb * dv), lambda b, h, t: (0, h)),
                  state],
        out_specs=[pl.BlockSpec((tb, hb * dv), lambda b, h, t: (b * nt + t, h)), state],
        out_shape=[jax.ShapeDtypeStruct((batch * seq, d), _mixer_dtype(chunk)),
                   jax.ShapeDtypeStruct(s0.shape, F32)],
        scratch_shapes=[pltpu.VMEM((hb, dv, dk), F32)],
        compiler_params=_params(("parallel", "parallel", "arbitrary")),
        name="hgrn",
    )(proj, proj, proj, proj, lower_bounds, norm_g.reshape(1, d), s0)
    return y, s1


def _topk_rows(s, k, payload=None):
    rows = lax.broadcasted_iota(jnp.int32, s.shape, 0)
    vals, picks = [], []
    for _ in range(k):
        m = jnp.max(s, axis=0, keepdims=True)
        pos = jnp.min(jnp.where(s == m, rows, s.shape[0]), axis=0, keepdims=True)
        hit = rows == pos
        vals.append(m)
        picks.append(pos if payload is None else jnp.max(jnp.where(hit, payload, -1), axis=0, keepdims=True))
        s = jnp.where(hit, -jnp.inf, s)
    return jnp.concatenate(vals, axis=0), jnp.concatenate(picks, axis=0)


def _route_kernel(q_ref, keys_ref, idx_ref, gate_ref, *, n_keys):
    half = q_ref.shape[1] // 2
    for g in range(q_ref.shape[0] // LANES):
        cols = slice(g * LANES, (g + 1) * LANES)
        qb = q_ref[cols, :].astype(BF16)
        tops = []
        for p in range(2):
            st = lax.dot_general(keys_ref[p].astype(BF16), qb[:, p * half:(p + 1) * half], _NT,
                                 preferred_element_type=F32)
            tops.append(_topk_rows(st, P_TOPK))
        (s1, i1), (s2, i2) = tops
        width = [P_TOPK // (a + 1) for a in range(P_TOPK)]
        pad = -sum(width) % SUBLANES
        cand = jnp.concatenate([s1[a:a + 1] + s2[0:width[a]] for a in range(P_TOPK)]
                               + [jnp.full((pad, LANES), -jnp.inf, F32)], axis=0)
        cidx = jnp.concatenate([i1[a:a + 1] * n_keys + i2[0:width[a]] for a in range(P_TOPK)]
                               + [jnp.full((pad, LANES), -1, jnp.int32)], axis=0)
        best, experts = _topk_rows(cand, P_TOPK, payload=cidx)
        ex = jnp.exp(best - best[0:1])
        idx_ref[:, cols] = experts
        gate_ref[:, cols] = ex / jnp.sum(ex, axis=0, keepdims=True)


def _route(q, sub_keys, layer):
    m = q.shape[0]
    _, heads, _, n_keys, half = sub_keys.shape
    tb = _pick(m, 512)
    out = pl.BlockSpec((None, P_TOPK, tb), lambda i, h: (h, 0, i))
    return pl.pallas_call(
        functools.partial(_route_kernel, n_keys=n_keys),
        grid=(m // tb, heads),
        in_specs=[pl.BlockSpec((tb, 2 * half), lambda i, h: (i, h)),
                  pl.BlockSpec((None, None, 2, n_keys, half), lambda i, h: (layer, h, 0, 0, 0))],
        out_specs=[out, out],
        out_shape=[jax.ShapeDtypeStruct((heads, P_TOPK, m), jnp.int32),
                   jax.ShapeDtypeStruct((heads, P_TOPK, m), F32)],
        compiler_params=_params(("parallel", "parallel")),
        name="route",
    )(q, sub_keys)


EVAL_TOKENS = 8
EVAL_UP_CHUNKS = 8


def _spread(total, parts):
    return [total * (j + 1) // parts - total * j // parts for j in range(parts)]


PACK_EXPERTS = 64


def _pack_kernel(u_ref, v_ref, o_ref):
    te, _, s_rows, _ = o_ref.shape
    o_ref[:, 0] = u_ref[...].reshape(te, s_rows, LANES).astype(BF16)
    o_ref[:, 1] = v_ref[...].reshape(te, s_rows, LANES).astype(BF16)


def _pack_tables(u_tab, v_tab):
    layers, experts, d = u_tab.shape
    te = _pick(experts, PACK_EXPERTS)
    s_rows = d // LANES
    src = pl.BlockSpec((None, te, d), lambda l, i: (l, i, 0))
    return pl.pallas_call(
        _pack_kernel,
        grid=(layers, experts // te),
        in_specs=[src, src],
        out_specs=pl.BlockSpec((None, te, 2, s_rows, LANES), lambda l, i: (l, i, 0, 0, 0)),
        out_shape=jax.ShapeDtypeStruct((layers, experts, 2, s_rows, LANES), BF16),
        compiler_params=_params(("parallel", "parallel")),
        name="pack_tables",
    )(u_tab, v_tab)


def _eval_kernel(idx0_ref, idx1_ref, x_ref, gate_ref, uv_hbm, y_ref, buf, sem, *, layer, n_picks):
    i = pl.program_id(0)
    n_steps = pl.num_programs(0)
    tb = idx1_ref.shape[0]
    s_rows = buf.shape[3]
    group = LANES // s_rows
    n_tiles = n_picks // group

    def issue(idx_ref, row, t, dst, k0, k1):
        for k in range(k0, k1):
            e = idx_ref[row, k]
            p = t * n_picks + k
            pltpu.make_async_copy(uv_hbm.at[layer, e], buf.at[dst, p], sem.at[dst]).start(priority=k % 2)

    def wait_slot(s):
        pltpu.make_async_copy(buf.at[s], buf.at[s], sem.at[s]).wait()

    @pl.when(i == 0)
    def _():
        for t in range(tb):
            issue(idx0_ref, t, t, 0, 0, n_picks)

    lane_s = lax.broadcasted_iota(jnp.int32, (s_rows, LANES), 1)
    row_s = lax.broadcasted_iota(jnp.int32, (s_rows, LANES), 0)
    diag = (lane_s % s_rows == row_s).astype(F32)
    sr = lax.broadcasted_iota(jnp.int32, (n_picks, n_picks * SUBLANES), 0)
    sk = lax.broadcasted_iota(jnp.int32, (n_picks, n_picks * SUBLANES), 1) // SUBLANES
    sub_sum = ((sr % n_tiles == sk // group) & (sr // n_tiles == sk % group)).astype(BF16)
    lane_sum = (lax.broadcasted_iota(jnp.int32, (group * LANES, LANES), 0) // LANES
                == lax.broadcasted_iota(jnp.int32, (group * LANES, LANES), 1) // s_rows).astype(BF16)
    pick_lane = (lax.broadcasted_iota(jnp.int32, (n_picks, LANES), 0) % group
                 == lax.broadcasted_iota(jnp.int32, (n_picks, LANES), 1) // s_rows).astype(BF16)
    tile_of = (lax.broadcasted_iota(jnp.int32, (n_tiles, n_picks), 1) // group
               == lax.broadcasted_iota(jnp.int32, (n_tiles, n_picks), 0)).astype(F32)

    n_vchunks = min(EVAL_UP_CHUNKS, n_tiles)
    n_down = n_picks // 2
    n_mid = 3 * n_picks // 16
    plan_down = _spread(n_down, n_tiles)
    plan_mid = _spread(n_mid, 4)
    plan_up = _spread(n_picks - n_down - n_mid, n_vchunks)

    def evaluate(row, t, slot, next_idx_ref, next_row):
        r0 = t * n_picks
        started = [0]

        def feed(count):
            issue(next_idx_ref, next_row, t, 1 - slot, started[0], started[0] + count)
            started[0] += count

        xf = jnp.concatenate([x_all[row:row + 1, s * LANES:(s + 1) * LANES] for s in range(s_rows)], axis=0)
        parts = []
        for c in range(n_tiles):
            feed(plan_down[c])
            prod = buf[slot, pl.ds(r0 + c * group, group), 0].astype(F32) * xf[None]
            parts.append(prod.reshape(group, s_rows // SUBLANES, SUBLANES, LANES).sum(axis=1)
                         .reshape(group * SUBLANES, LANES))
        pm = jnp.concatenate(parts, axis=0)
        hi = pm.astype(BF16)
        lo = (pm - hi.astype(F32)).astype(BF16)
        feed(plan_mid[0])
        s1 = jnp.dot(sub_sum, hi, preferred_element_type=F32) + jnp.dot(sub_sum, lo, preferred_element_type=F32)
        s1 = jnp.concatenate([s1[g * n_tiles:(g + 1) * n_tiles] for g in range(group)], axis=1)
        feed(plan_mid[1])
        act = _dot_by_mask(s1, lane_sum)
        feed(plan_mid[2])
        gate = _dot_by_mask(tile_of * gate_ref[row:row + 1, :], pick_lane)
        a = 0.5 * act * (1.0 + lax.erf(act * (2.0 ** -0.5))) * gate
        a_sel = jnp.concatenate([diag * a[c:c + 1, :] for c in range(n_tiles)], axis=1).astype(BF16)
        feed(plan_mid[3])
        per = n_picks // n_vchunks
        y2 = None
        for j in range(n_vchunks):
            feed(plan_up[j])
            v_j = buf[slot, pl.ds(r0 + j * per, per), 1].reshape(per * s_rows, LANES)
            part = jnp.dot(a_sel[:, j * per * s_rows:(j + 1) * per * s_rows], v_j, preferred_element_type=F32)
            y2 = part if y2 is None else y2 + part
        assert started[0] == n_picks
        for s in range(s_rows):
            y_ref[row:row + 1, s * LANES:(s + 1) * LANES] = y2[s:s + 1, :]

    x_all = x_ref[...].astype(F32)

    wait_slot(0)
    for t in range(tb):
        evaluate(t, t, 0, idx0_ref, tb + t)
    wait_slot(1)
    for t in range(tb):
        evaluate(tb + t, t, 1, idx1_ref, t)

    @pl.when(i == n_steps - 1)
    def _():
        wait_slot(0)


def _peer_eval(x, idx, gate, uv_pack, layer):
    m, d = x.shape
    n_picks = idx.shape[1]
    s_rows = d // LANES
    tb = EVAL_TOKENS
    n = m // (2 * tb)
    xs = pl.BlockSpec((2 * tb, d), lambda i: (i, 0))
    return pl.pallas_call(
        functools.partial(_eval_kernel, layer=layer, n_picks=n_picks),
        grid=(n,),
        in_specs=[pl.BlockSpec((2 * tb, n_picks), lambda i: (i, 0), memory_space=pltpu.SMEM),
                  pl.BlockSpec((tb, n_picks), lambda i: (jnp.minimum(2 * i + 2, 2 * n - 1), 0),
                               memory_space=pltpu.SMEM),
                  xs,
                  pl.BlockSpec((2 * tb, n_picks), lambda i: (i, 0)),
                  pl.BlockSpec(memory_space=pl.ANY)],
        out_specs=xs,
        out_shape=jax.ShapeDtypeStruct((m, d), F32),
        scratch_shapes=[pltpu.VMEM((2, tb * n_picks, 2, s_rows, LANES), BF16),
                        pltpu.SemaphoreType.DMA((2,))],
        compiler_params=_params(("arbitrary",)),
        name="peer_eval",
    )(idx, idx, x, gate, uv_pack)


def _peer(h, w_query, sub_keys, uv_pack, layer):
    m = h.shape[0]
    q = _matmul(h, w_query, layer)
    idx_t, gate_t = _route(q, sub_keys, layer)
    idx = idx_t.reshape(-1, m).T
    gate = gate_t.reshape(-1, m).T
    return _peer_eval(h, idx, gate, uv_pack, layer)


def _trunk(x, mod, seq_off, mc, mn, mm, hs, ln_g, ln_b, m_w_in, w_gate, b_gate, m_norm_g, m_w_out,
           h_w_in, h_lower_bounds, h_norm_g, h_w_out, p_w_query, p_sub_keys, uv_pack):
    batch, seq, d = x.shape
    depth = mod.shape[0]
    alpha = (2 * depth) ** 0.25
    rows = batch * seq
    out_c, out_n, out_m, out_s = [], [], [], []
    hin = _modulate(x, mod, 0, seq_off)
    for li in range(depth):
        j = li // 2
        hin2 = hin
        if li % 2 == 0:
            n_main = m_w_in.shape[2] - 2 * m_norm_g.shape[1]
            proj = _matmul(hin2, m_w_in, j, n_cols=n_main)
            gates = _matmul(hin2, w_gate, j)
            y, c1, n1, m1 = _mlstm(proj, gates, b_gate[j:j + 1], m_norm_g[j], mc[j], mn[j], mm[j], batch, seq)
            mix = _matmul(y, m_w_out, j)
            out_c.append(c1)
            out_n.append(n1)
            out_m.append(m1)
        else:
            proj = _matmul(hin2, h_w_in, j)
            y, s1 = _hgrn(proj, h_lower_bounds, li, h_norm_g[j], hs[j], batch, seq)
            mix = _matmul(y, h_w_out, j)
            out_s.append(s1)
        x, hin = _res_ln(x, mix.reshape(batch, seq, d), mod, ln_g, ln_b, li, 0, seq_off, alpha, (li, 4, 3))
        ff = _peer(hin, p_w_query, p_sub_keys, uv_pack, li)
        nxt = (li + 1, 1, 0) if li + 1 < depth else None
        x, hin = _res_ln(x, ff.reshape(batch, seq, d), mod, ln_g, ln_b, li, 1, seq_off, alpha, nxt)
    return x, jnp.stack(out_c), jnp.stack(out_n), jnp.stack(out_m), jnp.stack(out_s)


def kernel(x_prompt, x_sample, c_prompt, c_sample, state_mlstm_C, state_mlstm_n, state_mlstm_m, state_hgrn_S, ada_w, ada_b, ln_g, ln_b, m_w_in, m_b_gate, m_norm_g, m_w_out, h_w_in, h_lower_bounds, h_norm_g, h_w_out, p_w_query, p_sub_keys, p_u, p_v):
    depth, d, _ = ada_w.shape
    n_a, m_heads, m_dv = m_norm_g.shape
    n_b = h_norm_g.shape[0]
    batch = x_prompt.shape[0]
    dec_batch = x_sample.shape[0]

    n_seq = dec_batch + batch
    pad = -n_seq % SUBLANES
    c_all = jnp.concatenate([c_sample, c_prompt, jnp.zeros((pad, d), F32)], axis=0)
    mod = _ada(c_all, ada_w, ada_b).reshape(depth, n_seq + pad, 1, 6 * d)

    n_main = m_w_in.shape[2] - 2 * m_heads
    w_gate = jnp.pad(m_w_in[:, :, n_main:], ((0, 0), (0, 0), (0, LANES - 2 * m_heads)))
    b_gate = jnp.pad(m_b_gate, ((0, 0), (0, LANES - 2 * m_heads)))
    uv_pack = _pack_tables(p_u, p_v)

    m_dk = state_mlstm_C.shape[3]
    _, _, h_heads, h_dk, h_dv = state_hgrn_S.shape
    z_c = jnp.zeros((n_a, batch, m_heads, m_dk, m_dv), F32)
    z_n = jnp.zeros((n_a, batch, m_heads, m_dk), F32)
    z_m = jnp.zeros((n_a, batch, m_heads), F32)
    z_s = jnp.zeros((n_b, batch, h_heads, h_dk, h_dv), F32)

    shared = (ln_g, ln_b, m_w_in, w_gate, b_gate, m_norm_g, m_w_out, h_w_in, h_lower_bounds, h_norm_g, h_w_out,
              p_w_query, p_sub_keys, uv_pack)
    y_p, p_c, p_n, p_m, p_s = _trunk(x_prompt, mod, dec_batch, z_c, z_n, z_m, z_s, *shared)
    y_s, s_c, s_n, s_m, s_s = _trunk(x_sample, mod, 0, state_mlstm_C, state_mlstm_n, state_mlstm_m, state_hgrn_S,
                                     *shared)
    return (y_p, y_s, p_c, p_n, p_m, p_s, s_c, s_n, s_m, s_s)
```

```python
import functools

import jax
import jax.numpy as jnp
from jax import lax
from jax.experimental import pallas as pl
from jax.experimental.pallas import tpu as pltpu

F32 = jnp.float32
BF16 = jnp.bfloat16
HIGHEST = lax.Precision.HIGHEST

LN_EPS = 1e-5
P_TOPK = 16
M_CHUNK_PREF = 64
H_CHUNK_PREF = 32
LANES = 128
SUBLANES = 8
VMEM_LIMIT = 56 * 1024 * 1024

_NT = (((1,), (1,)), ((), ()))
_TN = (((0,), (0,)), ((), ()))


def _params(sem):
    return pltpu.CompilerParams(dimension_semantics=sem, vmem_limit_bytes=VMEM_LIMIT)


def _pick(n, pref):
    t = min(pref, n)
    while n % t:
        t //= 2
    return t


def _matmul_kernel(a_ref, w_ref, o_ref, acc_ref):
    k = pl.program_id(2)

    @pl.when(k == 0)
    def _():
        acc_ref[...] = jnp.zeros_like(acc_ref)

    acc_ref[...] += jnp.dot(a_ref[...].astype(BF16), w_ref[...].astype(BF16), preferred_element_type=F32)

    @pl.when(k == pl.num_programs(2) - 1)
    def _():
        o_ref[...] = acc_ref[...]


def _matmul(a, w, layer, n_cols=None):
    m, kdim = a.shape
    n = n_cols if n_cols is not None else w.shape[2]
    tm, tn, tk = _pick(m, 2048 if a.dtype == BF16 else 1024), _pick(n, 1024), _pick(kdim, 1024)
    return pl.pallas_call(
        _matmul_kernel,
        grid=(m // tm, n // tn, kdim // tk),
        in_specs=[pl.BlockSpec((tm, tk), lambda i, j, k: (i, k)),
                  pl.BlockSpec((None, tk, tn), lambda i, j, k: (layer, k, j))],
        out_specs=pl.BlockSpec((tm, tn), lambda i, j, k: (i, j)),
        out_shape=jax.ShapeDtypeStruct((m, n), F32),
        scratch_shapes=[pltpu.VMEM((tm, tn), F32)],
        compiler_params=_params(("parallel", "parallel", "arbitrary")),
        name="matmul",
    )(a, w)


def _ada_kernel(c_ref, w_ref, b_ref, o_ref, acc_ref):
    k = pl.program_id(2)

    @pl.when(k == 0)
    def _():
        acc_ref[...] = jnp.zeros_like(acc_ref)

    cs = jax.nn.silu(c_ref[...])
    acc_ref[...] += jnp.dot(cs.astype(BF16), w_ref[...].astype(BF16), preferred_element_type=F32)

    @pl.when(k == pl.num_programs(2) - 1)
    def _():
        o_ref[...] = acc_ref[...] + b_ref[...]


def _ada(c_all, ada_w, ada_b):
    depth, d, n = ada_w.shape
    rows = c_all.shape[0]
    tn, tk = _pick(n, 2048), _pick(d, 1024)
    return pl.pallas_call(
        _ada_kernel,
        grid=(depth, n // tn, d // tk),
        in_specs=[pl.BlockSpec((rows, tk), lambda l, j, k: (0, k)),
                  pl.BlockSpec((None, tk, tn), lambda l, j, k: (l, k, j)),
                  pl.BlockSpec((None, 1, tn), lambda l, j, k: (l, 0, j))],
        out_specs=pl.BlockSpec((None, rows, tn), lambda l, j, k: (l, 0, j)),
        out_shape=jax.ShapeDtypeStruct((depth, rows, n), F32),
        scratch_shapes=[pltpu.VMEM((rows, tn), F32)],
        compiler_params=_params(("parallel", "parallel", "arbitrary")),
        name="ada",
    )(c_all, ada_w, ada_b.reshape(depth, 1, n))


def _seq_blocks(x3):
    s, r, d = x3.shape
    if r >= 128:
        return 1, _pick(r, 128)
    return _pick(s, max(1, 128 // r)), r


def _mod_spec(layer, chunk, sb, seq_off, d):
    return pl.BlockSpec((None, sb, 1, d), lambda s, r: (layer, seq_off // sb + s, 0, chunk))


def _store_rows(o_ref, h):
    o_ref[...] = h.reshape(o_ref.shape).astype(o_ref.dtype)


def _rows_spec(sb, rb, r, d):
    return pl.BlockSpec((sb * rb, d), lambda i, j: (i * (r // rb) + j, 0))


def _modulate_kernel(x_ref, sc_ref, sh_ref, o_ref):
    _store_rows(o_ref, x_ref[...] * (1.0 + sc_ref[...]) + sh_ref[...])


def _modulate(x3, mod, layer, seq_off):
    s, r, d = x3.shape
    sb, rb = _seq_blocks(x3)
    xs = pl.BlockSpec((sb, rb, d), lambda i, j: (i, j, 0))
    return pl.pallas_call(
        _modulate_kernel,
        grid=(s // sb, r // rb),
        in_specs=[xs, _mod_spec(layer, 1, sb, seq_off, d), _mod_spec(layer, 0, sb, seq_off, d)],
        out_specs=_rows_spec(sb, rb, r, d),
        out_shape=jax.ShapeDtypeStruct((s * r, d), BF16),
        compiler_params=_params(("parallel", "parallel")),
        name="modulate",
    )(x3, mod, mod)


def _res_ln_kernel(x_ref, y_ref, g_ref, lg_ref, lb_ref, *rest, alpha, modulate):
    z = alpha * x_ref[...] + g_ref[...] * y_ref[...]
    mu = jnp.mean(z, axis=-1, keepdims=True)
    zc = z - mu
    var = jnp.mean(zc * zc, axis=-1, keepdims=True)
    xn = zc * lax.rsqrt(var + LN_EPS) * lg_ref[...] + lb_ref[...]
    if modulate:
        sc_ref, sh_ref, xo_ref, ho_ref = rest
        xo_ref[...] = xn
        _store_rows(ho_ref, xn * (1.0 + sc_ref[...]) + sh_ref[...])
    else:
        (xo_ref,) = rest
        xo_ref[...] = xn


def _res_ln(x3, y3, mod, ln_g, ln_b, layer, sub, seq_off, alpha, next_mod):
    s, r, d = x3.shape
    sb, rb = _seq_blocks(x3)
    xs = pl.BlockSpec((sb, rb, d), lambda i, j: (i, j, 0))
    ln_spec = pl.BlockSpec((None, 1, d), lambda i, j: (2 * layer + sub, 0, 0))
    lg = ln_g.reshape(-1, 1, d)
    lb = ln_b.reshape(-1, 1, d)
    in_specs = [xs, xs, _mod_spec(layer, 2 + 3 * sub, sb, seq_off, d), ln_spec, ln_spec]
    args = [x3, y3, mod, lg, lb]
    out_shape = [jax.ShapeDtypeStruct(x3.shape, F32)]
    out_specs = [xs]
    if next_mod is not None:
        nl, sc_chunk, sh_chunk = next_mod
        in_specs += [_mod_spec(nl, sc_chunk, sb, seq_off, d), _mod_spec(nl, sh_chunk, sb, seq_off, d)]
        args += [mod, mod]
        out_shape.append(jax.ShapeDtypeStruct((s * r, d), BF16))
        out_specs.append(_rows_spec(sb, rb, r, d))
    out = pl.pallas_call(
        functools.partial(_res_ln_kernel, alpha=alpha, modulate=next_mod is not None),
        grid=(s // sb, r // rb),
        in_specs=in_specs, out_specs=out_specs, out_shape=out_shape,
        compiler_params=_params(("parallel", "parallel")),
        name="res_ln",
    )(*args)
    return (out[0], out[1]) if next_mod is not None else (out[0], None)


def _mixer_dtype(chunk):
    return BF16 if chunk % (2 * SUBLANES) == 0 else F32


def _tril(n):
    r = lax.broadcasted_iota(jnp.int32, (n, n), 0)
    c = lax.broadcasted_iota(jnp.int32, (n, n), 1)
    return (r >= c).astype(F32)


def _dot_by_mask(x, mask):
    hi = x.astype(BF16)
    r1 = x - hi.astype(F32)
    mid = r1.astype(BF16)
    lo = (r1 - mid.astype(F32)).astype(BF16)
    return (jnp.dot(hi, mask, preferred_element_type=F32) + jnp.dot(mid, mask, preferred_element_type=F32)
            + jnp.dot(lo, mask, preferred_element_type=F32))


def _mlstm_kernel(q_ref, k_ref, v_ref, o_ref, g_ref, bg_ref, ng_ref, c0_ref, n0_ref, m0_ref,
                  y_ref, c_ref, n_ref, m_ref, *, heads, dk):
    h = pl.program_id(1)

    @pl.when(pl.program_id(2) == 0)
    def _():
        c_ref[...] = c0_ref[...]
        n_ref[...] = n0_ref[...]
        m_ref[...] = m0_ref[...]

    length = q_ref.shape[0]
    gates = g_ref[...] + bg_ref[...]
    lane = lax.broadcasted_iota(jnp.int32, gates.shape, 1)
    ig_col = jnp.sum(jnp.where(lane == h, gates, 0.0), axis=1, keepdims=True)
    lf_col = jax.nn.log_sigmoid(jnp.sum(jnp.where(lane == h + heads, gates, 0.0), axis=1, keepdims=True))
    ig_b = jnp.broadcast_to(ig_col, (length, LANES))
    lf_b = jnp.broadcast_to(lf_col, (length, LANES))
    tri = _tril(length)
    b_b = jnp.dot(tri, lf_b, precision=HIGHEST, preferred_element_type=F32)
    b_col = b_b[:, 0:1]
    sel = (lax.broadcasted_iota(jnp.int32, (SUBLANES, LANES), 1) == 0).astype(F32)
    b_row = lax.dot_general(sel, b_b, _NT, precision=HIGHEST, preferred_element_type=F32)[0:1]
    ig_row = lax.dot_general(sel, ig_b, _NT, precision=HIGHEST, preferred_element_type=F32)[0:1]

    m_prev = m_ref[...]
    causal = tri > 0.5
    dmat = jnp.where(causal, b_col - b_row + ig_row, -jnp.inf)
    inter = b_col + m_prev
    m_t = jnp.maximum(inter, jnp.max(dmat, axis=1, keepdims=True))
    w = jnp.exp(dmat - m_t)
    a_inter = jnp.exp(inter - m_t)

    q = q_ref[...] * (dk ** -0.5)
    k = k_ref[...]
    v = v_ref[...]
    qb, kb, vb = q.astype(BF16), k.astype(BF16), v.astype(BF16)
    cmat = c_ref[...]
    nvec = n_ref[...]
    s = lax.dot_general(qb, kb, _NT, preferred_element_type=F32) * w
    num = a_inter * jnp.dot(qb, cmat.astype(BF16), preferred_element_type=F32) \
        + jnp.dot(s.astype(BF16), vb, preferred_element_type=F32)
    den = a_inter * jnp.sum(q * nvec, axis=1, keepdims=True) + jnp.sum(s, axis=1, keepdims=True)
    hh = num / jnp.maximum(jnp.abs(den), jnp.exp(-m_t))

    mu = jnp.mean(hh, axis=1, keepdims=True)
    hc = hh - mu
    var = jnp.mean(hc * hc, axis=1, keepdims=True)
    hn = hc * lax.rsqrt(var + LN_EPS) * ng_ref[...]
    y_ref[...] = (jax.nn.sigmoid(o_ref[...]) * hn).astype(y_ref.dtype)

    m_new = m_t[length - 1:length, :]
    carry = jnp.exp(inter[length - 1:length, :] - m_new)
    src = jnp.exp(b_col[length - 1:length, :] - b_col + ig_col - m_new)
    kt = k * src
    c_ref[...] = carry * cmat + lax.dot_general(kt.astype(BF16), vb, _TN, preferred_element_type=F32)
    n_ref[...] = carry * nvec + jnp.sum(kt, axis=0, keepdims=True)
    m_ref[...] = m_new


def _mlstm(proj, gates, b_gate, norm_g, c0, n0, m0, batch, seq):
    _, heads, dk, dv = c0.shape
    chunk = _pick(seq, 256) if seq % M_CHUNK_PREF == 0 else seq
    nc = seq // chunk
    kq = heads
    kv = 2 * heads * dk // dv
    ko = kv + heads

    def rows(b, h, c):
        return b * nc + c

    state_c = pl.BlockSpec((None, None, dk, dv), lambda b, h, c: (b, h, 0, 0))
    state_n = pl.BlockSpec((None, None, 1, dk), lambda b, h, c: (b, h, 0, 0))
    state_m = pl.BlockSpec((None, None, 1, 1), lambda b, h, c: (b, h, 0, 0))
    y, c1, n1, m1 = pl.pallas_call(
        functools.partial(_mlstm_kernel, heads=heads, dk=dk),
        grid=(batch, heads, nc),
        in_specs=[pl.BlockSpec((chunk, dk), lambda b, h, c: (rows(b, h, c), h)),
                  pl.BlockSpec((chunk, dk), lambda b, h, c: (rows(b, h, c), kq + h)),
                  pl.BlockSpec((chunk, dv), lambda b, h, c: (rows(b, h, c), kv + h)),
                  pl.BlockSpec((chunk, dv), lambda b, h, c: (rows(b, h, c), ko + h)),
                  pl.BlockSpec((chunk, LANES), lambda b, h, c: (rows(b, h, c), 0)),
                  pl.BlockSpec((1, LANES), lambda b, h, c: (0, 0)),
                  pl.BlockSpec((None, 1, dv), lambda b, h, c: (h, 0, 0)),
                  state_c, state_n, state_m],
        out_specs=[pl.BlockSpec((chunk, dv), lambda b, h, c: (rows(b, h, c), h)), state_c, state_n, state_m],
        out_shape=[jax.ShapeDtypeStruct((batch * seq, heads * dv), _mixer_dtype(chunk)),
                   jax.ShapeDtypeStruct((batch, heads, dk, dv), F32),
                   jax.ShapeDtypeStruct((batch, heads, 1, dk), F32),
                   jax.ShapeDtypeStruct((batch, heads, 1, 1), F32)],
        compiler_params=_params(("parallel", "parallel", "arbitrary")),
        name="mlstm",
    )(proj, proj, proj, proj, gates, b_gate, norm_g.reshape(heads, 1, dv),
      c0, n0.reshape(batch, heads, 1, dk), m0.reshape(batch, heads, 1, 1))
    return y, c1, n1.reshape(batch, heads, dk), m1.reshape(batch, heads)


def _hgrn_kernel(q_ref, f_ref, i_ref, g_ref, lbw_ref, ng_ref, s0_ref, y_ref, s_ref, st_ref,
                 *, layer, chunk, hb):
    tb = pl.program_id(2)
    dh = LANES
    srow8 = lax.broadcasted_iota(jnp.int32, (SUBLANES, dh), 0)

    @pl.when(tb == 0)
    def _():
        for hh in range(hb):
            st_ref[hh] = s0_ref[hh].T

    lbw = lbw_ref[...]
    e = jnp.exp(lbw - jnp.max(lbw, axis=0, keepdims=True))
    soft = e / jnp.sum(e, axis=0, keepdims=True)
    lb_all = jnp.sum(soft[0:layer + 1], axis=0, keepdims=True) - soft[0:1]

    tri = _tril(chunk)
    n_chunks = q_ref.shape[0] // chunk

    def chunk_body(ci, carry):
        r0 = pl.multiple_of(ci * chunk, chunk)
        for hh in range(hb):
            cs = slice(hh * dh, (hh + 1) * dh)
            lb = lb_all[:, cs]
            fg = lb + (1.0 - lb) * jax.nn.sigmoid(f_ref[pl.ds(r0, chunk), cs])
            kk = 1.0 - fg
            logf = jnp.log(fg)
            qx = q_ref[pl.ds(r0, chunk), cs]
            qa = qx * jax.nn.sigmoid(qx)
            v = i_ref[pl.ds(r0, chunk), cs]
            bc = jnp.dot(tri, logf, precision=HIGHEST, preferred_element_type=F32)
            st = st_ref[hh]
            o = lax.dot_general((qa * jnp.exp(bc)).astype(BF16), st.astype(BF16), _NT, preferred_element_type=F32)
            rows_out = []
            for t in range(chunk):
                lo = (t // SUBLANES) * SUBLANES
                nb = lo + SUBLANES
                rel = jnp.where(srow8 <= t - lo, bc[t:t + 1, :] - bc[lo:nb, :], -jnp.inf)
                if lo:
                    rel = jnp.concatenate([bc[t:t + 1, :] - bc[0:lo, :], rel], axis=0)
                p = qa[t:t + 1, :] * kk[0:nb, :] * jnp.exp(rel)
                a = jnp.sum(p, axis=1, keepdims=True)
                rows_out.append(jnp.sum(a * v[0:nb, :], axis=0, keepdims=True))
            o = o + jnp.concatenate(rows_out, axis=0)
            b_end = bc[chunk - 1:chunk, :]
            kt = kk * jnp.exp(b_end - bc)
            st_ref[hh] = st * jnp.exp(b_end) + lax.dot_general(v.astype(BF16), kt.astype(BF16), _TN,
                                                               preferred_element_type=F32)
            on = o * lax.rsqrt(jnp.mean(o * o, axis=1, keepdims=True) + LN_EPS) * ng_ref[:, cs]
            gx = g_ref[pl.ds(r0, chunk), cs]
            y_ref[pl.ds(r0, chunk), cs] = (on * (gx * jax.nn.sigmoid(gx))).astype(y_ref.dtype)
        return carry

    lax.fori_loop(0, n_chunks, chunk_body, 0)

    @pl.when(tb == pl.num_programs(2) - 1)
    def _():
        for hh in range(hb):
            s_ref[hh] = st_ref[hh].T


def _hgrn(proj, lower_bounds, layer, norm_g, s0, batch, seq):
    _, heads, dk, dv = s0.shape
    assert dk == LANES and dv == LANES
    d = heads * dk
    chunk = H_CHUNK_PREF if seq % H_CHUNK_PREF == 0 else seq
    tb = _pick(seq, 256) if seq % H_CHUNK_PREF == 0 else seq
    hb = _pick(heads, 8)
    ng = heads // hb
    nt = seq // tb

    def col(part):
        return pl.BlockSpec((tb, hb * dk), lambda b, h, t: (b * nt + t, part * ng + h))

    state = pl.BlockSpec((None, hb, dk, dv), lambda b, h, t: (b, h, 0, 0))
    depth = lower_bounds.shape[0]
    y, s1 = pl.pallas_call(
        functools.partial(_hgrn_kernel, layer=layer, chunk=chunk, hb=hb),
        grid=(batch, ng, nt),
        in_specs=[col(0), col(1), col(2), col(3),
                  pl.BlockSpec((depth, hb * dk), lambda b, h, t: (0, h)),
                  pl.BlockSpec((1, hb * dv), lambda b, h, t: (0, h)),
                  state],
        out_specs=[pl.BlockSpec((tb, hb * dv), lambda b, h, t: (b * nt + t, h)), state],
        out_shape=[jax.ShapeDtypeStruct((batch * seq, d), _mixer_dtype(chunk)),
                   jax.ShapeDtypeStruct(s0.shape, F32)],
        scratch_shapes=[pltpu.VMEM((hb, dv, dk), F32)],
        compiler_params=_params(("parallel", "parallel", "arbitrary")),
        name="hgrn",
    )(proj, proj, proj, proj, lower_bounds, norm_g.reshape(1, d), s0)
    return y, s1


def _topk_rows(s, k, payload=None):
    rows = lax.broadcasted_iota(jnp.int32, s.shape, 0)
    vals, picks = [], []
    for _ in range(k):
        m = jnp.max(s, axis=0, keepdims=True)
        pos = jnp.min(jnp.where(s == m, rows, s.shape[0]), axis=0, keepdims=True)
        hit = rows == pos
        vals.append(m)
        picks.append(pos if payload is None else jnp.max(jnp.where(hit, payload, -1), axis=0, keepdims=True))
        s = jnp.where(hit, -jnp.inf, s)
    return jnp.concatenate(vals, axis=0), jnp.concatenate(picks, axis=0)


def _route_kernel(q_ref, keys_ref, idx_ref, gate_ref, *, n_keys):
    half = q_ref.shape[1] // 2
    for g in range(q_ref.shape[0] // LANES):
        cols = slice(g * LANES, (g + 1) * LANES)
        qb = q_ref[cols, :].astype(BF16)
        tops = []
        for p in range(2):
            st = lax.dot_general(keys_ref[p].astype(BF16), qb[:, p * half:(p + 1) * half], _NT,
                                 preferred_element_type=F32)
            tops.append(_topk_rows(st, P_TOPK))
        (s1, i1), (s2, i2) = tops
        width = [P_TOPK // (a + 1) for a in range(P_TOPK)]
        pad = -sum(width) % SUBLANES
        cand = jnp.concatenate([s1[a:a + 1] + s2[0:width[a]] for a in range(P_TOPK)]
                               + [jnp.full((pad, LANES), -jnp.inf, F32)], axis=0)
        cidx = jnp.concatenate([i1[a:a + 1] * n_keys + i2[0:width[a]] for a in range(P_TOPK)]
                               + [jnp.full((pad, LANES), -1, jnp.int32)], axis=0)
        best, experts = _topk_rows(cand, P_TOPK, payload=cidx)
        ex = jnp.exp(best - best[0:1])
        idx_ref[:, cols] = experts
        gate_ref[:, cols] = ex / jnp.sum(ex, axis=0, keepdims=True)


def _route(q, sub_keys, layer):
    m = q.shape[0]
    _, heads, _, n_keys, half = sub_keys.shape
    tb = _pick(m, 512)
    out = pl.BlockSpec((None, P_TOPK, tb), lambda i, h: (h, 0, i))
    return pl.pallas_call(
        functools.partial(_route_kernel, n_keys=n_keys),
        grid=(m // tb, heads),
        in_specs=[pl.BlockSpec((tb, 2 * half), lambda i, h: (i, h)),
                  pl.BlockSpec((None, None, 2, n_keys, half), lambda i, h: (layer, h, 0, 0, 0))],
        out_specs=[out, out],
        out_shape=[jax.ShapeDtypeStruct((heads, P_TOPK, m), jnp.int32),
                   jax.ShapeDtypeStruct((heads, P_TOPK, m), F32)],
        compiler_params=_params(("parallel", "parallel")),
        name="route",
    )(q, sub_keys)


EVAL_TOKENS = 8


PACK_EXPERTS = 64


def _pack_kernel(u_ref, v_ref, o_ref):
    te, _, s_rows, _ = o_ref.shape
    o_ref[:, 0] = u_ref[...].reshape(te, s_rows, LANES).astype(BF16)
    o_ref[:, 1] = v_ref[...].reshape(te, s_rows, LANES).astype(BF16)


def _pack_tables(u_tab, v_tab):
    layers, experts, d = u_tab.shape
    te = _pick(experts, PACK_EXPERTS)
    s_rows = d // LANES
    src = pl.BlockSpec((None, te, d), lambda l, i: (l, i, 0))
    return pl.pallas_call(
        _pack_kernel,
        grid=(layers, experts // te),
        in_specs=[src, src],
        out_specs=pl.BlockSpec((None, te, 2, s_rows, LANES), lambda l, i: (l, i, 0, 0, 0)),
        out_shape=jax.ShapeDtypeStruct((layers, experts, 2, s_rows, LANES), BF16),
        compiler_params=_params(("parallel", "parallel")),
        name="pack_tables",
    )(u_tab, v_tab)


def _eval_kernel(idx0_ref, idx1_ref, x_ref, gate_ref, uv_hbm, y_ref, buf, sem, *, layer, n_picks):
    i = pl.program_id(0)
    n_steps = pl.num_programs(0)
    tb = idx1_ref.shape[0]
    s_rows = buf.shape[3]
    group = LANES // s_rows
    n_tiles = n_picks // group

    def issue(idx_ref, row, t, dst, k0, k1):
        for k in range(k0, k1):
            e = idx_ref[row, k]
            p = t * n_picks + k
            pltpu.make_async_copy(uv_hbm.at[layer, e], buf.at[dst, p], sem.at[dst]).start(priority=k % 2)

    def wait_slot(s):
        pltpu.make_async_copy(buf.at[s], buf.at[s], sem.at[s]).wait()

    @pl.when(i == 0)
    def _():
        for t in range(tb):
            issue(idx0_ref, t, t, 0, 0, n_picks)

    lane_s = lax.broadcasted_iota(jnp.int32, (s_rows, LANES), 1)
    row_s = lax.broadcasted_iota(jnp.int32, (s_rows, LANES), 0)
    diag = (lane_s % s_rows == row_s).astype(F32)
    sr = lax.broadcasted_iota(jnp.int32, (n_picks, n_picks * SUBLANES), 0)
    sk = lax.broadcasted_iota(jnp.int32, (n_picks, n_picks * SUBLANES), 1) // SUBLANES
    sub_sum = ((sr % n_tiles == sk // group) & (sr // n_tiles == sk % group)).astype(BF16)
    lane_sum = (lax.broadcasted_iota(jnp.int32, (group * LANES, LANES), 0) // LANES
                == lax.broadcasted_iota(jnp.int32, (group * LANES, LANES), 1) // s_rows).astype(BF16)
    pick_lane = (lax.broadcasted_iota(jnp.int32, (n_picks, LANES), 0) % group
                 == lax.broadcasted_iota(jnp.int32, (n_picks, LANES), 1) // s_rows).astype(BF16)
    tile_of = (lax.broadcasted_iota(jnp.int32, (n_tiles, n_picks), 1) // group
               == lax.broadcasted_iota(jnp.int32, (n_tiles, n_picks), 0)).astype(F32)

    def products(row, t, slot, next_idx_ref, next_row):
        r0 = t * n_picks
        xf = jnp.concatenate([x_all[row:row + 1, s * LANES:(s + 1) * LANES] for s in range(s_rows)], axis=0)
        parts = []
        for c in range(n_tiles):
            issue(next_idx_ref, next_row, t, 1 - slot, c * group, (c + 1) * group)
            prod = buf[slot, pl.ds(r0 + c * group, group), 0].astype(F32) * xf[None]
            parts.append(prod.reshape(group, s_rows // SUBLANES, SUBLANES, LANES).sum(axis=1)
                         .reshape(group * SUBLANES, LANES))
        return (jnp.concatenate(parts, axis=0).astype(BF16),)

    def finish(row, t, slot, pm):
        r0 = t * n_picks
        s1 = jnp.dot(sub_sum, pm, preferred_element_type=F32)
        s1 = jnp.concatenate([s1[g * n_tiles:(g + 1) * n_tiles] for g in range(group)], axis=1)
        act = _dot_by_mask(s1, lane_sum)
        gate = _dot_by_mask(tile_of * gate_ref[row:row + 1, :], pick_lane)
        a = 0.5 * act * (1.0 + lax.erf(act * (2.0 ** -0.5))) * gate
        a_sel = jnp.concatenate([diag * a[c:c + 1, :] for c in range(n_tiles)], axis=1)
        v2 = buf[slot, pl.ds(r0, n_picks), 1].reshape(n_picks * s_rows, LANES)
        y2 = jnp.dot(a_sel.astype(BF16), v2, preferred_element_type=F32)
        for s in range(s_rows):
            y_ref[row:row + 1, s * LANES:(s + 1) * LANES] = y2[s:s + 1, :]

    def run_block(first_row, slot, next_idx_ref, next_first_row):
        pending = None
        for t in range(tb):
            sums = products(first_row + t, t, slot, next_idx_ref, next_first_row + t)
            if pending is not None:
                finish(first_row + t - 1, t - 1, slot, *pending)
            pending = sums
        finish(first_row + tb - 1, tb - 1, slot, *pending)

    x_all = x_ref[...].astype(F32)

    wait_slot(0)
    run_block(0, 0, idx0_ref, tb)
    wait_slot(1)
    run_block(tb, 1, idx1_ref, 0)

    @pl.when(i == n_steps - 1)
    def _():
        wait_slot(0)


def _peer_eval(x, idx, gate, uv_pack, layer):
    m, d = x.shape
    n_picks = idx.shape[1]
    s_rows = d // LANES
    tb = EVAL_TOKENS
    n = m // (2 * tb)
    xs = pl.BlockSpec((2 * tb, d), lambda i: (i, 0))
    return pl.pallas_call(
        functools.partial(_eval_kernel, layer=layer, n_picks=n_picks),
        grid=(n,),
        in_specs=[pl.BlockSpec((2 * tb, n_picks), lambda i: (i, 0), memory_space=pltpu.SMEM),
                  pl.BlockSpec((tb, n_picks), lambda i: (jnp.minimum(2 * i + 2, 2 * n - 1), 0),
                               memory_space=pltpu.SMEM),
                  xs,
                  pl.BlockSpec((2 * tb, n_picks), lambda i: (i, 0)),
                  pl.BlockSpec(memory_space=pl.ANY)],
        out_specs=xs,
        out_shape=jax.ShapeDtypeStruct((m, d), F32),
        scratch_shapes=[pltpu.VMEM((2, tb * n_picks, 2, s_rows, LANES), BF16),
                        pltpu.SemaphoreType.DMA((2,))],
        compiler_params=_params(("arbitrary",)),
        name="peer_eval",
    )(idx, idx, x, gate, uv_pack)


def _peer(h, w_query, sub_keys, uv_pack, layer):
    m = h.shape[0]
    q = _matmul(h, w_query, layer)
    idx_t, gate_t = _route(q, sub_keys, layer)
    idx = idx_t.reshape(-1, m).T
    gate = gate_t.reshape(-1, m).T
    return _peer_eval(h, idx, gate, uv_pack, layer)


def _trunk(x, mod, seq_off, mc, mn, mm, hs, ln_g, ln_b, m_w_in, w_gate, b_gate, m_norm_g, m_w_out,
           h_w_in, h_lower_bounds, h_norm_g, h_w_out, p_w_query, p_sub_keys, uv_pack):
    batch, seq, d = x.shape
    depth = mod.shape[0]
    alpha = (2 * depth) ** 0.25
    rows = batch * seq
    out_c, out_n, out_m, out_s = [], [], [], []
    hin = _modulate(x, mod, 0, seq_off)
    for li in range(depth):
        j = li // 2
        hin2 = hin
        if li % 2 == 0:
            n_main = m_w_in.shape[2] - 2 * m_norm_g.shape[1]
            proj = _matmul(hin2, m_w_in, j, n_cols=n_main)
            gates = _matmul(hin2, w_gate, j)
            y, c1, n1, m1 = _mlstm(proj, gates, b_gate[j:j + 1], m_norm_g[j], mc[j], mn[j], mm[j], batch, seq)
            mix = _matmul(y, m_w_out, j)
            out_c.append(c1)
            out_n.append(n1)
            out_m.append(m1)
        else:
            proj = _matmul(hin2, h_w_in, j)
            y, s1 = _hgrn(proj, h_lower_bounds, li, h_norm_g[j], hs[j], batch, seq)
            mix = _matmul(y, h_w_out, j)
            out_s.append(s1)
        x, hin = _res_ln(x, mix.reshape(batch, seq, d), mod, ln_g, ln_b, li, 0, seq_off, alpha, (li, 4, 3))
        ff = _peer(hin, p_w_query, p_sub_keys, uv_pack, li)
        nxt = (li + 1, 1, 0) if li + 1 < depth else None
        x, hin = _res_ln(x, ff.reshape(batch, seq, d), mod, ln_g, ln_b, li, 1, seq_off, alpha, nxt)
    return x, jnp.stack(out_c), jnp.stack(out_n), jnp.stack(out_m), jnp.stack(out_s)


def kernel(x_prompt, x_sample, c_prompt, c_sample, state_mlstm_C, state_mlstm_n, state_mlstm_m, state_hgrn_S, ada_w, ada_b, ln_g, ln_b, m_w_in, m_b_gate, m_norm_g, m_w_out, h_w_in, h_lower_bounds, h_norm_g, h_w_out, p_w_query, p_sub_keys, p_u, p_v):
    depth, d, _ = ada_w.shape
    n_a, m_heads, m_dv = m_norm_g.shape
    n_b = h_norm_g.shape[0]
    batch = x_prompt.shape[0]
    dec_batch = x_sample.shape[0]

    n_seq = dec_batch + batch
    pad = -n_seq % SUBLANES
    c_all = jnp.concatenate([c_sample, c_prompt, jnp.zeros((pad, d), F32)], axis=0)
    mod = _ada(c_all, ada_w, ada_b).reshape(depth, n_seq + pad, 1, 6 * d)

    n_main = m_w_in.shape[2] - 2 * m_heads
    w_gate = jnp.pad(m_w_in[:, :, n_main:], ((0, 0), (0, 0), (0, LANES - 2 * m_heads)))
    b_gate = jnp.pad(m_b_gate, ((0, 0), (0, LANES - 2 * m_heads)))
    uv_pack = _pack_tables(p_u, p_v)

    m_dk = state_mlstm_C.shape[3]
    _, _, h_heads, h_dk, h_dv = state_hgrn_S.shape
    z_c = jnp.zeros((n_a, batch, m_heads, m_dk, m_dv), F32)
    z_n = jnp.zeros((n_a, batch, m_heads, m_dk), F32)
    z_m = jnp.zeros((n_a, batch, m_heads), F32)
    z_s = jnp.zeros((n_b, batch, h_heads, h_dk, h_dv), F32)

    shared = (ln_g, ln_b, m_w_in, w_gate, b_gate, m_norm_g, m_w_out, h_w_in, h_lower_bounds, h_norm_g, h_w_out,
              p_w_query, p_sub_keys, uv_pack)
    y_p, p_c, p_n, p_m, p_s = _trunk(x_prompt, mod, dec_batch, z_c, z_n, z_m, z_s, *shared)
    y_s, s_c, s_n, s_m, s_s = _trunk(x_sample, mod, 0, state_mlstm_C, state_mlstm_n, state_mlstm_m, state_hgrn_S,
                                     *shared)
    return (y_p, y_s, p_c, p_n, p_m, p_s, s_c, s_n, s_m, s_s)
```

```python
import functools

import jax
import jax.numpy as jnp
from jax import lax
from jax.experimental import pallas as pl
from jax.experimental.pallas import tpu as pltpu

F32 = jnp.float32
BF16 = jnp.bfloat16
HIGHEST = lax.Precision.HIGHEST

LN_EPS = 1e-5
P_TOPK = 16
M_CHUNK_PREF = 64
H_CHUNK_PREF = 32
LANES = 128
SUBLANES = 8
VMEM_LIMIT = 56 * 1024 * 1024

_NT = (((1,), (1,)), ((), ()))
_TN = (((0,), (0,)), ((), ()))


def _params(sem):
    return pltpu.CompilerParams(dimension_semantics=sem, vmem_limit_bytes=VMEM_LIMIT)


def _pick(n, pref):
    t = min(pref, n)
    while n % t:
        t //= 2
    return t


def _matmul_kernel(a_ref, w_ref, o_ref, acc_ref):
    k = pl.program_id(2)

    @pl.when(k == 0)
    def _():
        acc_ref[...] = jnp.zeros_like(acc_ref)

    acc_ref[...] += jnp.dot(a_ref[...].astype(BF16), w_ref[...].astype(BF16), preferred_element_type=F32)

    @pl.when(k == pl.num_programs(2) - 1)
    def _():
        o_ref[...] = acc_ref[...]


def _matmul(a, w, layer, n_cols=None):
    m, kdim = a.shape
    n = n_cols if n_cols is not None else w.shape[2]
    tm, tn, tk = _pick(m, 2048 if a.dtype == BF16 else 1024), _pick(n, 1024), _pick(kdim, 1024)
    return pl.pallas_call(
        _matmul_kernel,
        grid=(m // tm, n // tn, kdim // tk),
        in_specs=[pl.BlockSpec((tm, tk), lambda i, j, k: (i, k)),
                  pl.BlockSpec((None, tk, tn), lambda i, j, k: (layer, k, j))],
        out_specs=pl.BlockSpec((tm, tn), lambda i, j, k: (i, j)),
        out_shape=jax.ShapeDtypeStruct((m, n), F32),
        scratch_shapes=[pltpu.VMEM((tm, tn), F32)],
        compiler_params=_params(("parallel", "parallel", "arbitrary")),
        name="matmul",
    )(a, w)


def _ada_kernel(c_ref, w_ref, b_ref, o_ref, acc_ref):
    k = pl.program_id(2)

    @pl.when(k == 0)
    def _():
        acc_ref[...] = jnp.zeros_like(acc_ref)

    cs = jax.nn.silu(c_ref[...])
    acc_ref[...] += jnp.dot(cs.astype(BF16), w_ref[...].astype(BF16), preferred_element_type=F32)

    @pl.when(k == pl.num_programs(2) - 1)
    def _():
        o_ref[...] = acc_ref[...] + b_ref[...]


def _ada(c_all, ada_w, ada_b):
    depth, d, n = ada_w.shape
    rows = c_all.shape[0]
    tn, tk = _pick(n, 2048), _pick(d, 1024)
    return pl.pallas_call(
        _ada_kernel,
        grid=(depth, n // tn, d // tk),
        in_specs=[pl.BlockSpec((rows, tk), lambda l, j, k: (0, k)),
                  pl.BlockSpec((None, tk, tn), lambda l, j, k: (l, k, j)),
                  pl.BlockSpec((None, 1, tn), lambda l, j, k: (l, 0, j))],
        out_specs=pl.BlockSpec((None, rows, tn), lambda l, j, k: (l, 0, j)),
        out_shape=jax.ShapeDtypeStruct((depth, rows, n), F32),
        scratch_shapes=[pltpu.VMEM((rows, tn), F32)],
        compiler_params=_params(("parallel", "parallel", "arbitrary")),
        name="ada",
    )(c_all, ada_w, ada_b.reshape(depth, 1, n))


def _seq_blocks(x3):
    s, r, d = x3.shape
    if r >= 128:
        return 1, _pick(r, 128)
    return _pick(s, max(1, 128 // r)), r


def _mod_spec(layer, chunk, sb, seq_off, d):
    return pl.BlockSpec((None, sb, 1, d), lambda s, r: (layer, seq_off // sb + s, 0, chunk))


def _store_rows(o_ref, h):
    o_ref[...] = h.reshape(o_ref.shape).astype(o_ref.dtype)


def _rows_spec(sb, rb, r, d):
    return pl.BlockSpec((sb * rb, d), lambda i, j: (i * (r // rb) + j, 0))


def _modulate_kernel(x_ref, sc_ref, sh_ref, o_ref):
    _store_rows(o_ref, x_ref[...] * (1.0 + sc_ref[...]) + sh_ref[...])


def _modulate(x3, mod, layer, seq_off):
    s, r, d = x3.shape
    sb, rb = _seq_blocks(x3)
    xs = pl.BlockSpec((sb, rb, d), lambda i, j: (i, j, 0))
    return pl.pallas_call(
        _modulate_kernel,
        grid=(s // sb, r // rb),
        in_specs=[xs, _mod_spec(layer, 1, sb, seq_off, d), _mod_spec(layer, 0, sb, seq_off, d)],
        out_specs=_rows_spec(sb, rb, r, d),
        out_shape=jax.ShapeDtypeStruct((s * r, d), BF16),
        compiler_params=_params(("parallel", "parallel")),
        name="modulate",
    )(x3, mod, mod)


def _res_ln_kernel(x_ref, y_ref, g_ref, lg_ref, lb_ref, *rest, alpha, modulate):
    z = alpha * x_ref[...] + g_ref[...] * y_ref[...]
    mu = jnp.mean(z, axis=-1, keepdims=True)
    zc = z - mu
    var = jnp.mean(zc * zc, axis=-1, keepdims=True)
    xn = zc * lax.rsqrt(var + LN_EPS) * lg_ref[...] + lb_ref[...]
    if modulate:
        sc_ref, sh_ref, xo_ref, ho_ref = rest
        xo_ref[...] = xn
        _store_rows(ho_ref, xn * (1.0 + sc_ref[...]) + sh_ref[...])
    else:
        (xo_ref,) = rest
        xo_ref[...] = xn


def _res_ln(x3, y3, mod, ln_g, ln_b, layer, sub, seq_off, alpha, next_mod):
    s, r, d = x3.shape
    sb, rb = _seq_blocks(x3)
    xs = pl.BlockSpec((sb, rb, d), lambda i, j: (i, j, 0))
    ln_spec = pl.BlockSpec((None, 1, d), lambda i, j: (2 * layer + sub, 0, 0))
    lg = ln_g.reshape(-1, 1, d)
    lb = ln_b.reshape(-1, 1, d)
    in_specs = [xs, xs, _mod_spec(layer, 2 + 3 * sub, sb, seq_off, d), ln_spec, ln_spec]
    args = [x3, y3, mod, lg, lb]
    out_shape = [jax.ShapeDtypeStruct(x3.shape, F32)]
    out_specs = [xs]
    if next_mod is not None:
        nl, sc_chunk, sh_chunk = next_mod
        in_specs += [_mod_spec(nl, sc_chunk, sb, seq_off, d), _mod_spec(nl, sh_chunk, sb, seq_off, d)]
        args += [mod, mod]
        out_shape.append(jax.ShapeDtypeStruct((s * r, d), BF16))
        out_specs.append(_rows_spec(sb, rb, r, d))
    out = pl.pallas_call(
        functools.partial(_res_ln_kernel, alpha=alpha, modulate=next_mod is not None),
        grid=(s // sb, r // rb),
        in_specs=in_specs, out_specs=out_specs, out_shape=out_shape,
        compiler_params=_params(("parallel", "parallel")),
        name="res_ln",
    )(*args)
    return (out[0], out[1]) if next_mod is not None else (out[0], None)


def _mixer_dtype(chunk):
    return BF16 if chunk % (2 * SUBLANES) == 0 else F32


def _tril(n):
    r = lax.broadcasted_iota(jnp.int32, (n, n), 0)
    c = lax.broadcasted_iota(jnp.int32, (n, n), 1)
    return (r >= c).astype(F32)


def _dot_by_mask(x, mask):
    hi = x.astype(BF16)
    r1 = x - hi.astype(F32)
    mid = r1.astype(BF16)
    lo = (r1 - mid.astype(F32)).astype(BF16)
    return (jnp.dot(hi, mask, preferred_element_type=F32) + jnp.dot(mid, mask, preferred_element_type=F32)
            + jnp.dot(lo, mask, preferred_element_type=F32))


def _mlstm_kernel(q_ref, k_ref, v_ref, o_ref, g_ref, bg_ref, ng_ref, c0_ref, n0_ref, m0_ref,
                  y_ref, c_ref, n_ref, m_ref, *, heads, dk):
    h = pl.program_id(1)

    @pl.when(pl.program_id(2) == 0)
    def _():
        c_ref[...] = c0_ref[...]
        n_ref[...] = n0_ref[...]
        m_ref[...] = m0_ref[...]

    length = q_ref.shape[0]
    gates = g_ref[...] + bg_ref[...]
    lane = lax.broadcasted_iota(jnp.int32, gates.shape, 1)
    ig_col = jnp.sum(jnp.where(lane == h, gates, 0.0), axis=1, keepdims=True)
    lf_col = jax.nn.log_sigmoid(jnp.sum(jnp.where(lane == h + heads, gates, 0.0), axis=1, keepdims=True))
    ig_b = jnp.broadcast_to(ig_col, (length, LANES))
    lf_b = jnp.broadcast_to(lf_col, (length, LANES))
    tri = _tril(length)
    b_b = jnp.dot(tri, lf_b, precision=HIGHEST, preferred_element_type=F32)
    b_col = b_b[:, 0:1]
    sel = (lax.broadcasted_iota(jnp.int32, (SUBLANES, LANES), 1) == 0).astype(F32)
    b_row = lax.dot_general(sel, b_b, _NT, precision=HIGHEST, preferred_element_type=F32)[0:1]
    ig_row = lax.dot_general(sel, ig_b, _NT, precision=HIGHEST, preferred_element_type=F32)[0:1]

    m_prev = m_ref[...]
    causal = tri > 0.5
    dmat = jnp.where(causal, b_col - b_row + ig_row, -jnp.inf)
    inter = b_col + m_prev
    m_t = jnp.maximum(inter, jnp.max(dmat, axis=1, keepdims=True))
    w = jnp.exp(dmat - m_t)
    a_inter = jnp.exp(inter - m_t)

    q = q_ref[...] * (dk ** -0.5)
    k = k_ref[...]
    v = v_ref[...]
    qb, kb, vb = q.astype(BF16), k.astype(BF16), v.astype(BF16)
    cmat = c_ref[...]
    nvec = n_ref[...]
    s = lax.dot_general(qb, kb, _NT, preferred_element_type=F32) * w
    num = a_inter * jnp.dot(qb, cmat.astype(BF16), preferred_element_type=F32) \
        + jnp.dot(s.astype(BF16), vb, preferred_element_type=F32)
    den = a_inter * jnp.sum(q * nvec, axis=1, keepdims=True) + jnp.sum(s, axis=1, keepdims=True)
    hh = num / jnp.maximum(jnp.abs(den), jnp.exp(-m_t))

    mu = jnp.mean(hh, axis=1, keepdims=True)
    hc = hh - mu
    var = jnp.mean(hc * hc, axis=1, keepdims=True)
    hn = hc * lax.rsqrt(var + LN_EPS) * ng_ref[...]
    y_ref[...] = (jax.nn.sigmoid(o_ref[...]) * hn).astype(y_ref.dtype)

    m_new = m_t[length - 1:length, :]
    carry = jnp.exp(inter[length - 1:length, :] - m_new)
    src = jnp.exp(b_col[length - 1:length, :] - b_col + ig_col - m_new)
    kt = k * src
    c_ref[...] = carry * cmat + lax.dot_general(kt.astype(BF16), vb, _TN, preferred_element_type=F32)
    n_ref[...] = carry * nvec + jnp.sum(kt, axis=0, keepdims=True)
    m_ref[...] = m_new


def _mlstm(proj, gates, b_gate, norm_g, c0, n0, m0, batch, seq):
    _, heads, dk, dv = c0.shape
    chunk = _pick(seq, 256) if seq % M_CHUNK_PREF == 0 else seq
    nc = seq // chunk
    kq = heads
    kv = 2 * heads * dk // dv
    ko = kv + heads

    def rows(b, h, c):
        return b * nc + c

    state_c = pl.BlockSpec((None, None, dk, dv), lambda b, h, c: (b, h, 0, 0))
    state_n = pl.BlockSpec((None, None, 1, dk), lambda b, h, c: (b, h, 0, 0))
    state_m = pl.BlockSpec((None, None, 1, 1), lambda b, h, c: (b, h, 0, 0))
    y, c1, n1, m1 = pl.pallas_call(
        functools.partial(_mlstm_kernel, heads=heads, dk=dk),
        grid=(batch, heads, nc),
        in_specs=[pl.BlockSpec((chunk, dk), lambda b, h, c: (rows(b, h, c), h)),
                  pl.BlockSpec((chunk, dk), lambda b, h, c: (rows(b, h, c), kq + h)),
                  pl.BlockSpec((chunk, dv), lambda b, h, c: (rows(b, h, c), kv + h)),
                  pl.BlockSpec((chunk, dv), lambda b, h, c: (rows(b, h, c), ko + h)),
                  pl.BlockSpec((chunk, LANES), lambda b, h, c: (rows(b, h, c), 0)),
                  pl.BlockSpec((1, LANES), lambda b, h, c: (0, 0)),
                  pl.BlockSpec((None, 1, dv), lambda b, h, c: (h, 0, 0)),
                  state_c, state_n, state_m],
        out_specs=[pl.BlockSpec((chunk, dv), lambda b, h, c: (rows(b, h, c), h)), state_c, state_n, state_m],
        out_shape=[jax.ShapeDtypeStruct((batch * seq, heads * dv), _mixer_dtype(chunk)),
                   jax.ShapeDtypeStruct((batch, heads, dk, dv), F32),
                   jax.ShapeDtypeStruct((batch, heads, 1, dk), F32),
                   jax.ShapeDtypeStruct((batch, heads, 1, 1), F32)],
        compiler_params=_params(("parallel", "parallel", "arbitrary")),
        name="mlstm",
    )(proj, proj, proj, proj, gates, b_gate, norm_g.reshape(heads, 1, dv),
      c0, n0.reshape(batch, heads, 1, dk), m0.reshape(batch, heads, 1, 1))
    return y, c1, n1.reshape(batch, heads, dk), m1.reshape(batch, heads)


def _hgrn_kernel(q_ref, f_ref, i_ref, g_ref, lbw_ref, ng_ref, s0_ref, y_ref, s_ref, st_ref,
                 *, layer, chunk, hb):
    tb = pl.program_id(2)
    dh = LANES
    srow8 = lax.broadcasted_iota(jnp.int32, (SUBLANES, dh), 0)

    @pl.when(tb == 0)
    def _():
        for hh in range(hb):
            st_ref[hh] = s0_ref[hh].T

    lbw = lbw_ref[...]
    e = jnp.exp(lbw - jnp.max(lbw, axis=0, keepdims=True))
    soft = e / jnp.sum(e, axis=0, keepdims=True)
    lb_all = jnp.sum(soft[0:layer + 1], axis=0, keepdims=True) - soft[0:1]

    tri = _tril(chunk)
    n_chunks = q_ref.shape[0] // chunk

    def chunk_body(ci, carry):
        r0 = pl.multiple_of(ci * chunk, chunk)
        for hh in range(hb):
            cs = slice(hh * dh, (hh + 1) * dh)
            lb = lb_all[:, cs]
            fg = lb + (1.0 - lb) * jax.nn.sigmoid(f_ref[pl.ds(r0, chunk), cs])
            kk = 1.0 - fg
            logf = jnp.log(fg)
            qx = q_ref[pl.ds(r0, chunk), cs]
            qa = qx * jax.nn.sigmoid(qx)
            v = i_ref[pl.ds(r0, chunk), cs]
            bc = jnp.dot(tri, logf, precision=HIGHEST, preferred_element_type=F32)
            st = st_ref[hh]
            o = lax.dot_general((qa * jnp.exp(bc)).astype(BF16), st.astype(BF16), _NT, preferred_element_type=F32)
            rows_out = []
            for t in range(chunk):
                lo = (t // SUBLANES) * SUBLANES
                nb = lo + SUBLANES
                rel = jnp.where(srow8 <= t - lo, bc[t:t + 1, :] - bc[lo:nb, :], -jnp.inf)
                if lo:
                    rel = jnp.concatenate([bc[t:t + 1, :] - bc[0:lo, :], rel], axis=0)
                p = qa[t:t + 1, :] * kk[0:nb, :] * jnp.exp(rel)
                a = jnp.sum(p, axis=1, keepdims=True)
                rows_out.append(jnp.sum(a * v[0:nb, :], axis=0, keepdims=True))
            o = o + jnp.concatenate(rows_out, axis=0)
            b_end = bc[chunk - 1:chunk, :]
            kt = kk * jnp.exp(b_end - bc)
            st_ref[hh] = st * jnp.exp(b_end) + lax.dot_general(v.astype(BF16), kt.astype(BF16), _TN,
                                                               preferred_element_type=F32)
            on = o * lax.rsqrt(jnp.mean(o * o, axis=1, keepdims=True) + LN_EPS) * ng_ref[:, cs]
            gx = g_ref[pl.ds(r0, chunk), cs]
            y_ref[pl.ds(r0, chunk), cs] = (on * (gx * jax.nn.sigmoid(gx))).astype(y_ref.dtype)
        return carry

    lax.fori_loop(0, n_chunks, chunk_body, 0)

    @pl.when(tb == pl.num_programs(2) - 1)
    def _():
        for hh in range(hb):
            s_ref[hh] = st_ref[hh].T


def _hgrn(proj, lower_bounds, layer, norm_g, s0, batch, seq):
    _, heads, dk, dv = s0.shape
    assert dk == LANES and dv == LANES
    d = heads * dk
    chunk = H_CHUNK_PREF if seq % H_CHUNK_PREF == 0 else seq
    tb = _pick(seq, 256) if seq % H_CHUNK_PREF == 0 else seq
    hb = _pick(heads, 8)
    ng = heads // hb
    nt = seq // tb

    def col(part):
        return pl.BlockSpec((tb, hb * dk), lambda b, h, t: (b * nt + t, part * ng + h))

    state = pl.BlockSpec((None, hb, dk, dv), lambda b, h, t: (b, h, 0, 0))
    depth = lower_bounds.shape[0]
    y, s1 = pl.pallas_call(
        functools.partial(_hgrn_kernel, layer=layer, chunk=chunk, hb=hb),
        grid=(batch, ng, nt),
        in_specs=[col(0), col(1), col(2), col(3),
                  pl.BlockSpec((depth, hb * dk), lambda b, h, t: (0, h)),
                  pl.BlockSpec((1, hb * dv), lambda b, h, t: (0, h)),
                  state],
        out_specs=[pl.BlockSpec((tb, hb * dv), lambda b, h, t: (b * nt + t, h)), state],
        out_shape=[jax.ShapeDtypeStruct((batch * seq, d), _mixer_dtype(chunk)),
                   jax.ShapeDtypeStruct(s0.shape, F32)],
        scratch_shapes=[pltpu.VMEM((hb, dv, dk), F32)],
        compiler_params=_params(("parallel", "parallel", "arbitrary")),
        name="hgrn",
    )(proj, proj, proj, proj, lower_bounds, norm_g.reshape(1, d), s0)
    return y, s1


def _topk_rows(s, k, payload=None):
    rows = lax.broadcasted_iota(jnp.int32, s.shape, 0)
    vals, picks = [], []
    for _ in range(k):
        m = jnp.max(s, axis=0, keepdims=True)
        pos = jnp.min(jnp.where(s == m, rows, s.shape[0]), axis=0, keepdims=True)
        hit = rows == pos
        vals.append(m)
        picks.append(pos if payload is None else jnp.max(jnp.where(hit, payload, -1), axis=0, keepdims=True))
        s = jnp.where(hit, -jnp.inf, s)
    return jnp.concatenate(vals, axis=0), jnp.concatenate(picks, axis=0)


def _route_kernel(q_ref, keys_ref, idx_ref, gate_ref, *, n_keys):
    half = q_ref.shape[1] // 2
    for g in range(q_ref.shape[0] // LANES):
        cols = slice(g * LANES, (g + 1) * LANES)
        qb = q_ref[cols, :].astype(BF16)
        tops = []
        for p in range(2):
            st = lax.dot_general(keys_ref[p].astype(BF16), qb[:, p * half:(p + 1) * half], _NT,
                                 preferred_element_type=F32)
            tops.append(_topk_rows(st, P_TOPK))
        (s1, i1), (s2, i2) = tops
        width = [P_TOPK // (a + 1) for a in range(P_TOPK)]
        pad = -sum(width) % SUBLANES
        cand = jnp.concatenate([s1[a:a + 1] + s2[0:width[a]] for a in range(P_TOPK)]
                               + [jnp.full((pad, LANES), -jnp.inf, F32)], axis=0)
        cidx = jnp.concatenate([i1[a:a + 1] * n_keys + i2[0:width[a]] for a in range(P_TOPK)]
                               + [jnp.full((pad, LANES), -1, jnp.int32)], axis=0)
        best, experts = _topk_rows(cand, P_TOPK, payload=cidx)
        ex = jnp.exp(best - best[0:1])
        idx_ref[:, cols] = experts
        gate_ref[:, cols] = ex / jnp.sum(ex, axis=0, keepdims=True)


def _route(q, sub_keys, layer):
    m = q.shape[0]
    _, heads, _, n_keys, half = sub_keys.shape
    tb = _pick(m, 512)
    out = pl.BlockSpec((None, P_TOPK, tb), lambda i, h: (h, 0, i))
    return pl.pallas_call(
        functools.partial(_route_kernel, n_keys=n_keys),
        grid=(m // tb, heads),
        in_specs=[pl.BlockSpec((tb, 2 * half), lambda i, h: (i, h)),
                  pl.BlockSpec((None, None, 2, n_keys, half), lambda i, h: (layer, h, 0, 0, 0))],
        out_specs=[out, out],
        out_shape=[jax.ShapeDtypeStruct((heads, P_TOPK, m), jnp.int32),
                   jax.ShapeDtypeStruct((heads, P_TOPK, m), F32)],
        compiler_params=_params(("parallel", "parallel")),
        name="route",
    )(q, sub_keys)


EVAL_TOKENS = 8


PACK_EXPERTS = 64


def _pack_kernel(u_ref, v_ref, o_ref):
    te, _, s_rows, _ = o_ref.shape
    o_ref[:, 0] = u_ref[...].reshape(te, s_rows, LANES).astype(BF16)
    o_ref[:, 1] = v_ref[...].reshape(te, s_rows, LANES).astype(BF16)


def _pack_tables(u_tab, v_tab):
    layers, experts, d = u_tab.shape
    te = _pick(experts, PACK_EXPERTS)
    s_rows = d // LANES
    src = pl.BlockSpec((None, te, d), lambda l, i: (l, i, 0))
    return pl.pallas_call(
        _pack_kernel,
        grid=(layers, experts // te),
        in_specs=[src, src],
        out_specs=pl.BlockSpec((None, te, 2, s_rows, LANES), lambda l, i: (l, i, 0, 0, 0)),
        out_shape=jax.ShapeDtypeStruct((layers, experts, 2, s_rows, LANES), BF16),
        compiler_params=_params(("parallel", "parallel")),
        name="pack_tables",
    )(u_tab, v_tab)


def _eval_kernel(idx0_ref, idx1_ref, x_ref, gate_ref, uv_hbm, y_ref, buf, sem, *, layer, n_picks):
    i = pl.program_id(0)
    n_steps = pl.num_programs(0)
    tb = idx1_ref.shape[0]
    s_rows = buf.shape[3]
    group = LANES // s_rows
    n_tiles = n_picks // group

    def issue(idx_ref, row, t, dst, k0, k1):
        for k in range(k0, k1):
            e = idx_ref[row, k]
            p = t * n_picks + k
            pltpu.make_async_copy(uv_hbm.at[layer, e], buf.at[dst, p], sem.at[dst]).start(priority=k % 2)

    def wait_slot(s):
        pltpu.make_async_copy(buf.at[s], buf.at[s], sem.at[s]).wait()

    @pl.when(i == 0)
    def _():
        for t in range(tb):
            issue(idx0_ref, t, t, 0, 0, n_picks)

    lane_s = lax.broadcasted_iota(jnp.int32, (s_rows, LANES), 1)
    row_s = lax.broadcasted_iota(jnp.int32, (s_rows, LANES), 0)
    diag = (lane_s % s_rows == row_s).astype(F32)
    sr = lax.broadcasted_iota(jnp.int32, (n_picks, n_picks * SUBLANES), 0)
    sk = lax.broadcasted_iota(jnp.int32, (n_picks, n_picks * SUBLANES), 1) // SUBLANES
    sub_sum = ((sr % n_tiles == sk // group) & (sr // n_tiles == sk % group)).astype(BF16)
    lane_sum = (lax.broadcasted_iota(jnp.int32, (group * LANES, LANES), 0) // LANES
                == lax.broadcasted_iota(jnp.int32, (group * LANES, LANES), 1) // s_rows).astype(BF16)
    pick_lane = (lax.broadcasted_iota(jnp.int32, (n_picks, LANES), 0) % group
                 == lax.broadcasted_iota(jnp.int32, (n_picks, LANES), 1) // s_rows).astype(BF16)
    tile_of = (lax.broadcasted_iota(jnp.int32, (n_tiles, n_picks), 1) // group
               == lax.broadcasted_iota(jnp.int32, (n_tiles, n_picks), 0)).astype(F32)

    def products(row, t, slot, next_idx_ref, next_row):
        r0 = t * n_picks
        xf = jnp.concatenate([x_all[row:row + 1, s * LANES:(s + 1) * LANES] for s in range(s_rows)], axis=0)
        parts = []
        for c in range(n_tiles):
            issue(next_idx_ref, next_row, t, 1 - slot, c * group, (c + 1) * group)
            prod = buf[slot, pl.ds(r0 + c * group, group), 0].astype(F32) * xf[None]
            parts.append(prod.reshape(group, s_rows // SUBLANES, SUBLANES, LANES).sum(axis=1)
                         .reshape(group * SUBLANES, LANES))
        return (jnp.concatenate(parts, axis=0).astype(BF16),)

    def finish(row, t, slot, pm):
        r0 = t * n_picks
        s1 = jnp.dot(sub_sum, pm, preferred_element_type=F32)
        s1 = jnp.concatenate([s1[g * n_tiles:(g + 1) * n_tiles] for g in range(group)], axis=1)
        act = _dot_by_mask(s1, lane_sum)
        gate = gate_all[row * n_tiles:(row + 1) * n_tiles]
        a = 0.5 * act * (1.0 + lax.erf(act * (2.0 ** -0.5))) * gate
        a_sel = jnp.concatenate([diag * a[c:c + 1, :] for c in range(n_tiles)], axis=1)
        v2 = buf[slot, pl.ds(r0, n_picks), 1].reshape(n_picks * s_rows, LANES)
        y2 = jnp.dot(a_sel.astype(BF16), v2, preferred_element_type=F32)
        for s in range(s_rows):
            y_ref[row:row + 1, s * LANES:(s + 1) * LANES] = y2[s:s + 1, :]

    def run_block(first_row, slot, next_idx_ref, next_first_row):
        pending = None
        for t in range(tb):
            sums = products(first_row + t, t, slot, next_idx_ref, next_first_row + t)
            if pending is not None:
                finish(first_row + t - 1, t - 1, slot, *pending)
            pending = sums
        finish(first_row + tb - 1, tb - 1, slot, *pending)

    x_all = x_ref[...].astype(F32)
    gate_all = _dot_by_mask(
        jnp.concatenate([tile_of * gate_ref[r:r + 1, :] for r in range(2 * tb)], axis=0), pick_lane)

    wait_slot(0)
    run_block(0, 0, idx0_ref, tb)
    wait_slot(1)
    run_block(tb, 1, idx1_ref, 0)

    @pl.when(i == n_steps - 1)
    def _():
        wait_slot(0)


def _peer_eval(x, idx, gate, uv_pack, layer):
    m, d = x.shape
    n_picks = idx.shape[1]
    s_rows = d // LANES
    tb = EVAL_TOKENS
    n = m // (2 * tb)
    xs = pl.BlockSpec((2 * tb, d), lambda i: (i, 0))
    return pl.pallas_call(
        functools.partial(_eval_kernel, layer=layer, n_picks=n_picks),
        grid=(n,),
        in_specs=[pl.BlockSpec((2 * tb, n_picks), lambda i: (i, 0), memory_space=pltpu.SMEM),
                  pl.BlockSpec((tb, n_picks), lambda i: (jnp.minimum(2 * i + 2, 2 * n - 1), 0),
                               memory_space=pltpu.SMEM),
                  xs,
                  pl.BlockSpec((2 * tb, n_picks), lambda i: (i, 0)),
                  pl.BlockSpec(memory_space=pl.ANY)],
        out_specs=xs,
        out_shape=jax.ShapeDtypeStruct((m, d), F32),
        scratch_shapes=[pltpu.VMEM((2, tb * n_picks, 2, s_rows, LANES), BF16),
                        pltpu.SemaphoreType.DMA((2,))],
        compiler_params=_params(("arbitrary",)),
        name="peer_eval",
    )(idx, idx, x, gate, uv_pack)


def _peer(h, w_query, sub_keys, uv_pack, layer):
    m = h.shape[0]
    q = _matmul(h, w_query, layer)
    idx_t, gate_t = _route(q, sub_keys, layer)
    idx = idx_t.reshape(-1, m).T
    gate = gate_t.reshape(-1, m).T
    return _peer_eval(h, idx, gate, uv_pack, layer)


def _trunk(x, mod, seq_off, mc, mn, mm, hs, ln_g, ln_b, m_w_in, w_gate, b_gate, m_norm_g, m_w_out,
           h_w_in, h_lower_bounds, h_norm_g, h_w_out, p_w_query, p_sub_keys, uv_pack):
    batch, seq, d = x.shape
    depth = mod.shape[0]
    alpha = (2 * depth) ** 0.25
    rows = batch * seq
    out_c, out_n, out_m, out_s = [], [], [], []
    hin = _modulate(x, mod, 0, seq_off)
    for li in range(depth):
        j = li // 2
        hin2 = hin
        if li % 2 == 0:
            n_main = m_w_in.shape[2] - 2 * m_norm_g.shape[1]
            proj = _matmul(hin2, m_w_in, j, n_cols=n_main)
            gates = _matmul(hin2, w_gate, j)
            y, c1, n1, m1 = _mlstm(proj, gates, b_gate[j:j + 1], m_norm_g[j], mc[j], mn[j], mm[j], batch, seq)
            mix = _matmul(y, m_w_out, j)
            out_c.append(c1)
            out_n.append(n1)
            out_m.append(m1)
        else:
            proj = _matmul(hin2, h_w_in, j)
            y, s1 = _hgrn(proj, h_lower_bounds, li, h_norm_g[j], hs[j], batch, seq)
            mix = _matmul(y, h_w_out, j)
            out_s.append(s1)
        x, hin = _res_ln(x, mix.reshape(batch, seq, d), mod, ln_g, ln_b, li, 0, seq_off, alpha, (li, 4, 3))
        ff = _peer(hin, p_w_query, p_sub_keys, uv_pack, li)
        nxt = (li + 1, 1, 0) if li + 1 < depth else None
        x, hin = _res_ln(x, ff.reshape(batch, seq, d), mod, ln_g, ln_b, li, 1, seq_off, alpha, nxt)
    return x, jnp.stack(out_c), jnp.stack(out_n), jnp.stack(out_m), jnp.stack(out_s)


def kernel(x_prompt, x_sample, c_prompt, c_sample, state_mlstm_C, state_mlstm_n, state_mlstm_m, state_hgrn_S, ada_w, ada_b, ln_g, ln_b, m_w_in, m_b_gate, m_norm_g, m_w_out, h_w_in, h_lower_bounds, h_norm_g, h_w_out, p_w_query, p_sub_keys, p_u, p_v):
    depth, d, _ = ada_w.shape
    n_a, m_heads, m_dv = m_norm_g.shape
    n_b = h_norm_g.shape[0]
    batch = x_prompt.shape[0]
    dec_batch = x_sample.shape[0]

    n_seq = dec_batch + batch
    pad = -n_seq % SUBLANES
    c_all = jnp.concatenate([c_sample, c_prompt, jnp.zeros((pad, d), F32)], axis=0)
    mod = _ada(c_all, ada_w, ada_b).reshape(depth, n_seq + pad, 1, 6 * d)

    n_main = m_w_in.shape[2] - 2 * m_heads
    w_gate = jnp.pad(m_w_in[:, :, n_main:], ((0, 0), (0, 0), (0, LANES - 2 * m_heads)))
    b_gate = jnp.pad(m_b_gate, ((0, 0), (0, LANES - 2 * m_heads)))
    uv_pack = _pack_tables(p_u, p_v)

    m_dk = state_mlstm_C.shape[3]
    _, _, h_heads, h_dk, h_dv = state_hgrn_S.shape
    z_c = jnp.zeros((n_a, batch, m_heads, m_dk, m_dv), F32)
    z_n = jnp.zeros((n_a, batch, m_heads, m_dk), F32)
    z_m = jnp.zeros((n_a, batch, m_heads), F32)
    z_s = jnp.zeros((n_b, batch, h_heads, h_dk, h_dv), F32)

    shared = (ln_g, ln_b, m_w_in, w_gate, b_gate, m_norm_g, m_w_out, h_w_in, h_lower_bounds, h_norm_g, h_w_out,
              p_w_query, p_sub_keys, uv_pack)
    y_p, p_c, p_n, p_m, p_s = _trunk(x_prompt, mod, dec_batch, z_c, z_n, z_m, z_s, *shared)
    y_s, s_c, s_n, s_m, s_s = _trunk(x_sample, mod, 0, state_mlstm_C, state_mlstm_n, state_mlstm_m, state_hgrn_S,
                                     *shared)
    return (y_p, y_s, p_c, p_n, p_m, p_s, s_c, s_n, s_m, s_s)
```

```python
import functools

import jax
import jax.numpy as jnp
from jax import lax
from jax.experimental import pallas as pl
from jax.experimental.pallas import tpu as pltpu

F32 = jnp.float32
BF16 = jnp.bfloat16
HIGHEST = lax.Precision.HIGHEST

LN_EPS = 1e-5
P_TOPK = 16
M_CHUNK_PREF = 64
H_CHUNK_PREF = 32
LANES = 128
SUBLANES = 8
VMEM_LIMIT = 56 * 1024 * 1024

_NT = (((1,), (1,)), ((), ()))
_TN = (((0,), (0,)), ((), ()))


def _params(sem):
    return pltpu.CompilerParams(dimension_semantics=sem, vmem_limit_bytes=VMEM_LIMIT)


def _pick(n, pref):
    t = min(pref, n)
    while n % t:
        t //= 2
    return t


def _matmul_kernel(a_ref, w_ref, o_ref, acc_ref):
    k = pl.program_id(2)

    @pl.when(k == 0)
    def _():
        acc_ref[...] = jnp.zeros_like(acc_ref)

    acc_ref[...] += jnp.dot(a_ref[...].astype(BF16), w_ref[...].astype(BF16), preferred_element_type=F32)

    @pl.when(k == pl.num_programs(2) - 1)
    def _():
        o_ref[...] = acc_ref[...]


def _matmul(a, w, layer, n_cols=None):
    m, kdim = a.shape
    n = n_cols if n_cols is not None else w.shape[2]
    tm, tn, tk = _pick(m, 2048 if a.dtype == BF16 else 1024), _pick(n, 1024), _pick(kdim, 1024)
    return pl.pallas_call(
        _matmul_kernel,
        grid=(m // tm, n // tn, kdim // tk),
        in_specs=[pl.BlockSpec((tm, tk), lambda i, j, k: (i, k)),
                  pl.BlockSpec((None, tk, tn), lambda i, j, k: (layer, k, j))],
        out_specs=pl.BlockSpec((tm, tn), lambda i, j, k: (i, j)),
        out_shape=jax.ShapeDtypeStruct((m, n), F32),
        scratch_shapes=[pltpu.VMEM((tm, tn), F32)],
        compiler_params=_params(("parallel", "parallel", "arbitrary")),
        name="matmul",
    )(a, w)


def _ada_kernel(c_ref, w_ref, b_ref, o_ref, acc_ref):
    k = pl.program_id(2)

    @pl.when(k == 0)
    def _():
        acc_ref[...] = jnp.zeros_like(acc_ref)

    cs = jax.nn.silu(c_ref[...])
    acc_ref[...] += jnp.dot(cs.astype(BF16), w_ref[...].astype(BF16), preferred_element_type=F32)

    @pl.when(k == pl.num_programs(2) - 1)
    def _():
        o_ref[...] = acc_ref[...] + b_ref[...]


def _ada(c_all, ada_w, ada_b):
    depth, d, n = ada_w.shape
    rows = c_all.shape[0]
    tn, tk = _pick(n, 2048), _pick(d, 1024)
    return pl.pallas_call(
        _ada_kernel,
        grid=(depth, n // tn, d // tk),
        in_specs=[pl.BlockSpec((rows, tk), lambda l, j, k: (0, k)),
                  pl.BlockSpec((None, tk, tn), lambda l, j, k: (l, k, j)),
                  pl.BlockSpec((None, 1, tn), lambda l, j, k: (l, 0, j))],
        out_specs=pl.BlockSpec((None, rows, tn), lambda l, j, k: (l, 0, j)),
        out_shape=jax.ShapeDtypeStruct((depth, rows, n), F32),
        scratch_shapes=[pltpu.VMEM((rows, tn), F32)],
        compiler_params=_params(("parallel", "parallel", "arbitrary")),
        name="ada",
    )(c_all, ada_w, ada_b.reshape(depth, 1, n))


def _seq_blocks(x3):
    s, r, d = x3.shape
    if r >= 128:
        return 1, _pick(r, 128)
    return _pick(s, max(1, 128 // r)), r


def _mod_spec(layer, chunk, sb, seq_off, d):
    return pl.BlockSpec((None, sb, 1, d), lambda s, r: (layer, seq_off // sb + s, 0, chunk))


def _store_rows(o_ref, h):
    o_ref[...] = h.reshape(o_ref.shape).astype(o_ref.dtype)


def _rows_spec(sb, rb, r, d):
    return pl.BlockSpec((sb * rb, d), lambda i, j: (i * (r // rb) + j, 0))


def _modulate_kernel(x_ref, sc_ref, sh_ref, o_ref):
    _store_rows(o_ref, x_ref[...] * (1.0 + sc_ref[...]) + sh_ref[...])


def _modulate(x3, mod, layer, seq_off):
    s, r, d = x3.shape
    sb, rb = _seq_blocks(x3)
    xs = pl.BlockSpec((sb, rb, d), lambda i, j: (i, j, 0))
    return pl.pallas_call(
        _modulate_kernel,
        grid=(s // sb, r // rb),
        in_specs=[xs, _mod_spec(layer, 1, sb, seq_off, d), _mod_spec(layer, 0, sb, seq_off, d)],
        out_specs=_rows_spec(sb, rb, r, d),
        out_shape=jax.ShapeDtypeStruct((s * r, d), BF16),
        compiler_params=_params(("parallel", "parallel")),
        name="modulate",
    )(x3, mod, mod)


def _res_ln_kernel(x_ref, y_ref, g_ref, lg_ref, lb_ref, *rest, alpha, modulate):
    z = alpha * x_ref[...] + g_ref[...] * y_ref[...]
    mu = jnp.mean(z, axis=-1, keepdims=True)
    zc = z - mu
    var = jnp.mean(zc * zc, axis=-1, keepdims=True)
    xn = zc * lax.rsqrt(var + LN_EPS) * lg_ref[...] + lb_ref[...]
    if modulate:
        sc_ref, sh_ref, xo_ref, ho_ref = rest
        xo_ref[...] = xn
        _store_rows(ho_ref, xn * (1.0 + sc_ref[...]) + sh_ref[...])
    else:
        (xo_ref,) = rest
        xo_ref[...] = xn


def _res_ln(x3, y3, mod, ln_g, ln_b, layer, sub, seq_off, alpha, next_mod):
    s, r, d = x3.shape
    sb, rb = _seq_blocks(x3)
    xs = pl.BlockSpec((sb, rb, d), lambda i, j: (i, j, 0))
    ln_spec = pl.BlockSpec((None, 1, d), lambda i, j: (2 * layer + sub, 0, 0))
    lg = ln_g.reshape(-1, 1, d)
    lb = ln_b.reshape(-1, 1, d)
    in_specs = [xs, xs, _mod_spec(layer, 2 + 3 * sub, sb, seq_off, d), ln_spec, ln_spec]
    args = [x3, y3, mod, lg, lb]
    out_shape = [jax.ShapeDtypeStruct(x3.shape, F32)]
    out_specs = [xs]
    if next_mod is not None:
        nl, sc_chunk, sh_chunk = next_mod
        in_specs += [_mod_spec(nl, sc_chunk, sb, seq_off, d), _mod_spec(nl, sh_chunk, sb, seq_off, d)]
        args += [mod, mod]
        out_shape.append(jax.ShapeDtypeStruct((s * r, d), BF16))
        out_specs.append(_rows_spec(sb, rb, r, d))
    out = pl.pallas_call(
        functools.partial(_res_ln_kernel, alpha=alpha, modulate=next_mod is not None),
        grid=(s // sb, r // rb),
        in_specs=in_specs, out_specs=out_specs, out_shape=out_shape,
        compiler_params=_params(("parallel", "parallel")),
        name="res_ln",
    )(*args)
    return (out[0], out[1]) if next_mod is not None else (out[0], None)


def _mixer_dtype(chunk):
    return BF16 if chunk % (2 * SUBLANES) == 0 else F32


def _tril(n):
    r = lax.broadcasted_iota(jnp.int32, (n, n), 0)
    c = lax.broadcasted_iota(jnp.int32, (n, n), 1)
    return (r >= c).astype(F32)


def _dot_by_mask(x, mask):
    hi = x.astype(BF16)
    r1 = x - hi.astype(F32)
    mid = r1.astype(BF16)
    lo = (r1 - mid.astype(F32)).astype(BF16)
    return (jnp.dot(hi, mask, preferred_element_type=F32) + jnp.dot(mid, mask, preferred_element_type=F32)
            + jnp.dot(lo, mask, preferred_element_type=F32))


def _mlstm_kernel(q_ref, k_ref, v_ref, o_ref, g_ref, bg_ref, ng_ref, c0_ref, n0_ref, m0_ref,
                  y_ref, c_ref, n_ref, m_ref, *, heads, dk):
    h = pl.program_id(1)

    @pl.when(pl.program_id(2) == 0)
    def _():
        c_ref[...] = c0_ref[...]
        n_ref[...] = n0_ref[...]
        m_ref[...] = m0_ref[...]

    length = q_ref.shape[0]
    gates = g_ref[...] + bg_ref[...]
    lane = lax.broadcasted_iota(jnp.int32, gates.shape, 1)
    ig_col = jnp.sum(jnp.where(lane == h, gates, 0.0), axis=1, keepdims=True)
    lf_col = jax.nn.log_sigmoid(jnp.sum(jnp.where(lane == h + heads, gates, 0.0), axis=1, keepdims=True))
    ig_b = jnp.broadcast_to(ig_col, (length, LANES))
    lf_b = jnp.broadcast_to(lf_col, (length, LANES))
    tri = _tril(length)
    b_b = jnp.dot(tri, lf_b, precision=HIGHEST, preferred_element_type=F32)
    b_col = b_b[:, 0:1]
    sel = (lax.broadcasted_iota(jnp.int32, (SUBLANES, LANES), 1) == 0).astype(F32)
    b_row = lax.dot_general(sel, b_b, _NT, precision=HIGHEST, preferred_element_type=F32)[0:1]
    ig_row = lax.dot_general(sel, ig_b, _NT, precision=HIGHEST, preferred_element_type=F32)[0:1]

    m_prev = m_ref[...]
    causal = tri > 0.5
    dmat = jnp.where(causal, b_col - b_row + ig_row, -jnp.inf)
    inter = b_col + m_prev
    m_t = jnp.maximum(inter, jnp.max(dmat, axis=1, keepdims=True))
    w = jnp.exp(dmat - m_t)
    a_inter = jnp.exp(inter - m_t)

    q = q_ref[...] * (dk ** -0.5)
    k = k_ref[...]
    v = v_ref[...]
    qb, kb, vb = q.astype(BF16), k.astype(BF16), v.astype(BF16)
    cmat = c_ref[...]
    nvec = n_ref[...]
    s = lax.dot_general(qb, kb, _NT, preferred_element_type=F32) * w
    num = a_inter * jnp.dot(qb, cmat.astype(BF16), preferred_element_type=F32) \
        + jnp.dot(s.astype(BF16), vb, preferred_element_type=F32)
    den = a_inter * jnp.sum(q * nvec, axis=1, keepdims=True) + jnp.sum(s, axis=1, keepdims=True)
    hh = num / jnp.maximum(jnp.abs(den), jnp.exp(-m_t))

    mu = jnp.mean(hh, axis=1, keepdims=True)
    hc = hh - mu
    var = jnp.mean(hc * hc, axis=1, keepdims=True)
    hn = hc * lax.rsqrt(var + LN_EPS) * ng_ref[...]
    y_ref[...] = (jax.nn.sigmoid(o_ref[...]) * hn).astype(y_ref.dtype)

    m_new = m_t[length - 1:length, :]
    carry = jnp.exp(inter[length - 1:length, :] - m_new)
    src = jnp.exp(b_col[length - 1:length, :] - b_col + ig_col - m_new)
    kt = k * src
    c_ref[...] = carry * cmat + lax.dot_general(kt.astype(BF16), vb, _TN, preferred_element_type=F32)
    n_ref[...] = carry * nvec + jnp.sum(kt, axis=0, keepdims=True)
    m_ref[...] = m_new


def _mlstm(proj, gates, b_gate, norm_g, c0, n0, m0, batch, seq):
    _, heads, dk, dv = c0.shape
    chunk = _pick(seq, 256) if seq % M_CHUNK_PREF == 0 else seq
    nc = seq // chunk
    kq = heads
    kv = 2 * heads * dk // dv
    ko = kv + heads

    def rows(b, h, c):
        return b * nc + c

    state_c = pl.BlockSpec((None, None, dk, dv), lambda b, h, c: (b, h, 0, 0))
    state_n = pl.BlockSpec((None, None, 1, dk), lambda b, h, c: (b, h, 0, 0))
    state_m = pl.BlockSpec((None, None, 1, 1), lambda b, h, c: (b, h, 0, 0))
    y, c1, n1, m1 = pl.pallas_call(
        functools.partial(_mlstm_kernel, heads=heads, dk=dk),
        grid=(batch, heads, nc),
        in_specs=[pl.BlockSpec((chunk, dk), lambda b, h, c: (rows(b, h, c), h)),
                  pl.BlockSpec((chunk, dk), lambda b, h, c: (rows(b, h, c), kq + h)),
                  pl.BlockSpec((chunk, dv), lambda b, h, c: (rows(b, h, c), kv + h)),
                  pl.BlockSpec((chunk, dv), lambda b, h, c: (rows(b, h, c), ko + h)),
                  pl.BlockSpec((chunk, LANES), lambda b, h, c: (rows(b, h, c), 0)),
                  pl.BlockSpec((1, LANES), lambda b, h, c: (0, 0)),
                  pl.BlockSpec((None, 1, dv), lambda b, h, c: (h, 0, 0)),
                  state_c, state_n, state_m],
        out_specs=[pl.BlockSpec((chunk, dv), lambda b, h, c: (rows(b, h, c), h)), state_c, state_n, state_m],
        out_shape=[jax.ShapeDtypeStruct((batch * seq, heads * dv), _mixer_dtype(chunk)),
                   jax.ShapeDtypeStruct((batch, heads, dk, dv), F32),
                   jax.ShapeDtypeStruct((batch, heads, 1, dk), F32),
                   jax.ShapeDtypeStruct((batch, heads, 1, 1), F32)],
        compiler_params=_params(("parallel", "parallel", "arbitrary")),
        name="mlstm",
    )(proj, proj, proj, proj, gates, b_gate, norm_g.reshape(heads, 1, dv),
      c0, n0.reshape(batch, heads, 1, dk), m0.reshape(batch, heads, 1, 1))
    return y, c1, n1.reshape(batch, heads, dk), m1.reshape(batch, heads)


def _hgrn_kernel(q_ref, f_ref, i_ref, g_ref, lbw_ref, ng_ref, s0_ref, y_ref, s_ref, st_ref,
                 *, layer, chunk, hb):
    tb = pl.program_id(2)
    dh = LANES
    srow8 = lax.broadcasted_iota(jnp.int32, (SUBLANES, dh), 0)

    @pl.when(tb == 0)
    def _():
        for hh in range(hb):
            st_ref[hh] = s0_ref[hh].T

    lbw = lbw_ref[...]
    e = jnp.exp(lbw - jnp.max(lbw, axis=0, keepdims=True))
    soft = e / jnp.sum(e, axis=0, keepdims=True)
    lb_all = jnp.sum(soft[0:layer + 1], axis=0, keepdims=True) - soft[0:1]

    tri = _tril(chunk)
    n_chunks = q_ref.shape[0] // chunk

    def chunk_body(ci, carry):
        r0 = pl.multiple_of(ci * chunk, chunk)
        for hh in range(hb):
            cs = slice(hh * dh, (hh + 1) * dh)
            lb = lb_all[:, cs]
            fg = lb + (1.0 - lb) * jax.nn.sigmoid(f_ref[pl.ds(r0, chunk), cs])
            kk = 1.0 - fg
            logf = jnp.log(fg)
            qx = q_ref[pl.ds(r0, chunk), cs]
            qa = qx * jax.nn.sigmoid(qx)
            v = i_ref[pl.ds(r0, chunk), cs]
            if chunk == SUBLANES:
                run = [logf[0:1, :]]
                for t in range(1, chunk):
                    run.append(run[-1] + logf[t:t + 1, :])
                bc = jnp.concatenate(run, axis=0)
            else:
                bc = jnp.dot(tri, logf, precision=HIGHEST, preferred_element_type=F32)
            st = st_ref[hh]
            o = lax.dot_general((qa * jnp.exp(bc)).astype(BF16), st.astype(BF16), _NT, preferred_element_type=F32)
            rows_out = []
            for t in range(chunk):
                lo = (t // SUBLANES) * SUBLANES
                nb = lo + SUBLANES
                rel = jnp.where(srow8 <= t - lo, bc[t:t + 1, :] - bc[lo:nb, :], -jnp.inf)
                if lo:
                    rel = jnp.concatenate([bc[t:t + 1, :] - bc[0:lo, :], rel], axis=0)
                p = qa[t:t + 1, :] * kk[0:nb, :] * jnp.exp(rel)
                a = jnp.sum(p, axis=1, keepdims=True)
                rows_out.append(jnp.sum(a * v[0:nb, :], axis=0, keepdims=True))
            o = o + jnp.concatenate(rows_out, axis=0)
            b_end = bc[chunk - 1:chunk, :]
            kt = kk * jnp.exp(b_end - bc)
            st_ref[hh] = st * jnp.exp(b_end) + lax.dot_general(v.astype(BF16), kt.astype(BF16), _TN,
                                                               preferred_element_type=F32)
            on = o * lax.rsqrt(jnp.mean(o * o, axis=1, keepdims=True) + LN_EPS) * ng_ref[:, cs]
            gx = g_ref[pl.ds(r0, chunk), cs]
            y_ref[pl.ds(r0, chunk), cs] = (on * (gx * jax.nn.sigmoid(gx))).astype(y_ref.dtype)
        return carry

    lax.fori_loop(0, n_chunks, chunk_body, 0)

    @pl.when(tb == pl.num_programs(2) - 1)
    def _():
        for hh in range(hb):
            s_ref[hh] = st_ref[hh].T


def _hgrn(proj, lower_bounds, layer, norm_g, s0, batch, seq):
    _, heads, dk, dv = s0.shape
    assert dk == LANES and dv == LANES
    d = heads * dk
    chunk = H_CHUNK_PREF if seq % H_CHUNK_PREF == 0 else seq
    tb = _pick(seq, 256) if seq % H_CHUNK_PREF == 0 else seq
    hb = _pick(heads, 8)
    ng = heads // hb
    nt = seq // tb

    def col(part):
        return pl.BlockSpec((tb, hb * dk), lambda b, h, t: (b * nt + t, part * ng + h))

    state = pl.BlockSpec((None, hb, dk, dv), lambda b, h, t: (b, h, 0, 0))
    depth = lower_bounds.shape[0]
    y, s1 = pl.pallas_call(
        functools.partial(_hgrn_kernel, layer=layer, chunk=chunk, hb=hb),
        grid=(batch, ng, nt),
        in_specs=[col(0), col(1), col(2), col(3),
                  pl.BlockSpec((depth, hb * dk), lambda b, h, t: (0, h)),
                  pl.BlockSpec((1, hb * dv), lambda b, h, t: (0, h)),
                  state],
        out_specs=[pl.BlockSpec((tb, hb * dv), lambda b, h, t: (b * nt + t, h)), state],
        out_shape=[jax.ShapeDtypeStruct((batch * seq, d), _mixer_dtype(chunk)),
                   jax.ShapeDtypeStruct(s0.shape, F32)],
        scratch_shapes=[pltpu.VMEM((hb, dv, dk), F32)],
        compiler_params=_params(("parallel", "parallel", "arbitrary")),
        name="hgrn",
    )(proj, proj, proj, proj, lower_bounds, norm_g.reshape(1, d), s0)
    return y, s1


def _topk_rows(s, k, payload=None):
    rows = lax.broadcasted_iota(jnp.int32, s.shape, 0)
    vals, picks = [], []
    for _ in range(k):
        m = jnp.max(s, axis=0, keepdims=True)
        pos = jnp.min(jnp.where(s == m, rows, s.shape[0]), axis=0, keepdims=True)
        hit = rows == pos
        vals.append(m)
        picks.append(pos if payload is None else jnp.max(jnp.where(hit, payload, -1), axis=0, keepdims=True))
        s = jnp.where(hit, -jnp.inf, s)
    return jnp.concatenate(vals, axis=0), jnp.concatenate(picks, axis=0)


def _route_kernel(q_ref, keys_ref, idx_ref, gate_ref, *, n_keys):
    half = q_ref.shape[1] // 2
    for g in range(q_ref.shape[0] // LANES):
        cols = slice(g * LANES, (g + 1) * LANES)
        qb = q_ref[cols, :].astype(BF16)
        tops = []
        for p in range(2):
            st = lax.dot_general(keys_ref[p].astype(BF16), qb[:, p * half:(p + 1) * half], _NT,
                                 preferred_element_type=F32)
            tops.append(_topk_rows(st, P_TOPK))
        (s1, i1), (s2, i2) = tops
        width = [P_TOPK // (a + 1) for a in range(P_TOPK)]
        pad = -sum(width) % SUBLANES
        cand = jnp.concatenate([s1[a:a + 1] + s2[0:width[a]] for a in range(P_TOPK)]
                               + [jnp.full((pad, LANES), -jnp.inf, F32)], axis=0)
        cidx = jnp.concatenate([i1[a:a + 1] * n_keys + i2[0:width[a]] for a in range(P_TOPK)]
                               + [jnp.full((pad, LANES), -1, jnp.int32)], axis=0)
        best, experts = _topk_rows(cand, P_TOPK, payload=cidx)
        ex = jnp.exp(best - best[0:1])
        idx_ref[:, cols] = experts
        gate_ref[:, cols] = ex / jnp.sum(ex, axis=0, keepdims=True)


def _route(q, sub_keys, layer):
    m = q.shape[0]
    _, heads, _, n_keys, half = sub_keys.shape
    tb = _pick(m, 512)
    out = pl.BlockSpec((None, P_TOPK, tb), lambda i, h: (h, 0, i))
    return pl.pallas_call(
        functools.partial(_route_kernel, n_keys=n_keys),
        grid=(m // tb, heads),
        in_specs=[pl.BlockSpec((tb, 2 * half), lambda i, h: (i, h)),
                  pl.BlockSpec((None, None, 2, n_keys, half), lambda i, h: (layer, h, 0, 0, 0))],
        out_specs=[out, out],
        out_shape=[jax.ShapeDtypeStruct((heads, P_TOPK, m), jnp.int32),
                   jax.ShapeDtypeStruct((heads, P_TOPK, m), F32)],
        compiler_params=_params(("parallel", "parallel")),
        name="route",
    )(q, sub_keys)


EVAL_TOKENS = 8


PACK_EXPERTS = 64


def _pack_kernel(u_ref, v_ref, o_ref):
    te, _, s_rows, _ = o_ref.shape
    o_ref[:, 0] = u_ref[...].reshape(te, s_rows, LANES).astype(BF16)
    o_ref[:, 1] = v_ref[...].reshape(te, s_rows, LANES).astype(BF16)


def _pack_tables(u_tab, v_tab):
    layers, experts, d = u_tab.shape
    te = _pick(experts, PACK_EXPERTS)
    s_rows = d // LANES
    src = pl.BlockSpec((None, te, d), lambda l, i: (l, i, 0))
    return pl.pallas_call(
        _pack_kernel,
        grid=(layers, experts // te),
        in_specs=[src, src],
        out_specs=pl.BlockSpec((None, te, 2, s_rows, LANES), lambda l, i: (l, i, 0, 0, 0)),
        out_shape=jax.ShapeDtypeStruct((layers, experts, 2, s_rows, LANES), BF16),
        compiler_params=_params(("parallel", "parallel")),
        name="pack_tables",
    )(u_tab, v_tab)


def _eval_kernel(idx0_ref, idx1_ref, x_ref, gate_ref, uv_hbm, y_ref, buf, sem, *, layer, n_picks):
    i = pl.program_id(0)
    n_steps = pl.num_programs(0)
    tb = idx1_ref.shape[0]
    s_rows = buf.shape[3]
    group = LANES // s_rows
    n_tiles = n_picks // group

    def issue(idx_ref, row, t, dst, k0, k1):
        for k in range(k0, k1):
            e = idx_ref[row, k]
            p = t * n_picks + k
            pltpu.make_async_copy(uv_hbm.at[layer, e], buf.at[dst, p], sem.at[dst]).start(priority=k % 2)

    def wait_slot(s):
        pltpu.make_async_copy(buf.at[s], buf.at[s], sem.at[s]).wait()

    @pl.when(i == 0)
    def _():
        for t in range(tb):
            issue(idx0_ref, t, t, 0, 0, n_picks)

    lane_s = lax.broadcasted_iota(jnp.int32, (s_rows, LANES), 1)
    row_s = lax.broadcasted_iota(jnp.int32, (s_rows, LANES), 0)
    diag = (lane_s % s_rows == row_s).astype(F32)
    sr = lax.broadcasted_iota(jnp.int32, (n_picks, n_picks * SUBLANES), 0)
    sk = lax.broadcasted_iota(jnp.int32, (n_picks, n_picks * SUBLANES), 1) // SUBLANES
    sub_sum = ((sr % n_tiles == sk // group) & (sr // n_tiles == sk % group)).astype(BF16)
    lane_sum = (lax.broadcasted_iota(jnp.int32, (group * LANES, LANES), 0) // LANES
                == lax.broadcasted_iota(jnp.int32, (group * LANES, LANES), 1) // s_rows).astype(BF16)
    pick_lane = (lax.broadcasted_iota(jnp.int32, (n_picks, LANES), 0) % group
                 == lax.broadcasted_iota(jnp.int32, (n_picks, LANES), 1) // s_rows).astype(BF16)
    tile_of = (lax.broadcasted_iota(jnp.int32, (n_tiles, n_picks), 1) // group
               == lax.broadcasted_iota(jnp.int32, (n_tiles, n_picks), 0)).astype(F32)

    def products(row, t, slot, next_idx_ref, next_row):
        r0 = t * n_picks
        xf = jnp.concatenate([x_all[row:row + 1, s * LANES:(s + 1) * LANES] for s in range(s_rows)], axis=0)
        parts = []
        for c in range(n_tiles):
            issue(next_idx_ref, next_row, t, 1 - slot, c * group, (c + 1) * group)
            prod = buf[slot, pl.ds(r0 + c * group, group), 0].astype(F32) * xf[None]
            parts.append(prod.reshape(group, s_rows // SUBLANES, SUBLANES, LANES).sum(axis=1)
                         .reshape(group * SUBLANES, LANES))
        return (jnp.concatenate(parts, axis=0).astype(BF16),)

    def finish(row, t, slot, pm):
        r0 = t * n_picks
        s1 = jnp.dot(sub_sum, pm, preferred_element_type=F32)
        s1 = jnp.concatenate([s1[g * n_tiles:(g + 1) * n_tiles] for g in range(group)], axis=1)
        act = _dot_by_mask(s1, lane_sum)
        gate = gate_all[row * n_tiles:(row + 1) * n_tiles]
        a = 0.5 * act * (1.0 + lax.erf(act * (2.0 ** -0.5))) * gate
        a_sel = jnp.concatenate([diag * a[c:c + 1, :] for c in range(n_tiles)], axis=1)
        v2 = buf[slot, pl.ds(r0, n_picks), 1].reshape(n_picks * s_rows, LANES)
        y2 = jnp.dot(a_sel.astype(BF16), v2, preferred_element_type=F32)
        for s in range(s_rows):
            y_ref[row:row + 1, s * LANES:(s + 1) * LANES] = y2[s:s + 1, :]

    def run_block(first_row, slot, next_idx_ref, next_first_row):
        pending = None
        for t in range(tb):
            sums = products(first_row + t, t, slot, next_idx_ref, next_first_row + t)
            if pending is not None:
                finish(first_row + t - 1, t - 1, slot, *pending)
            pending = sums
        finish(first_row + tb - 1, tb - 1, slot, *pending)

    x_all = x_ref[...].astype(F32)
    gate_all = _dot_by_mask(
        jnp.concatenate([tile_of * gate_ref[r:r + 1, :] for r in range(2 * tb)], axis=0), pick_lane)

    wait_slot(0)
    run_block(0, 0, idx0_ref, tb)
    wait_slot(1)
    run_block(tb, 1, idx1_ref, 0)

    @pl.when(i == n_steps - 1)
    def _():
        wait_slot(0)


def _peer_eval(x, idx, gate, uv_pack, layer):
    m, d = x.shape
    n_picks = idx.shape[1]
    s_rows = d // LANES
    tb = EVAL_TOKENS
    n = m // (2 * tb)
    xs = pl.BlockSpec((2 * tb, d), lambda i: (i, 0))
    return pl.pallas_call(
        functools.partial(_eval_kernel, layer=layer, n_picks=n_picks),
        grid=(n,),
        in_specs=[pl.BlockSpec((2 * tb, n_picks), lambda i: (i, 0), memory_space=pltpu.SMEM),
                  pl.BlockSpec((tb, n_picks), lambda i: (jnp.minimum(2 * i + 2, 2 * n - 1), 0),
                               memory_space=pltpu.SMEM),
                  xs,
                  pl.BlockSpec((2 * tb, n_picks), lambda i: (i, 0)),
                  pl.BlockSpec(memory_space=pl.ANY)],
        out_specs=xs,
        out_shape=jax.ShapeDtypeStruct((m, d), F32),
        scratch_shapes=[pltpu.VMEM((2, tb * n_picks, 2, s_rows, LANES), BF16),
                        pltpu.SemaphoreType.DMA((2,))],
        compiler_params=_params(("arbitrary",)),
        name="peer_eval",
    )(idx, idx, x, gate, uv_pack)


def _peer(h, w_query, sub_keys, uv_pack, layer):
    m = h.shape[0]
    q = _matmul(h, w_query, layer)
    idx_t, gate_t = _route(q, sub_keys, layer)
    idx = idx_t.reshape(-1, m).T
    gate = gate_t.reshape(-1, m).T
    return _peer_eval(h, idx, gate, uv_pack, layer)


def _trunk(x, mod, seq_off, mc, mn, mm, hs, ln_g, ln_b, m_w_in, w_gate, b_gate, m_norm_g, m_w_out,
           h_w_in, h_lower_bounds, h_norm_g, h_w_out, p_w_query, p_sub_keys, uv_pack):
    batch, seq, d = x.shape
    depth = mod.shape[0]
    alpha = (2 * depth) ** 0.25
    rows = batch * seq
    out_c, out_n, out_m, out_s = [], [], [], []
    hin = _modulate(x, mod, 0, seq_off)
    for li in range(depth):
        j = li // 2
        hin2 = hin
        if li % 2 == 0:
            n_main = m_w_in.shape[2] - 2 * m_norm_g.shape[1]
            proj = _matmul(hin2, m_w_in, j, n_cols=n_main)
            gates = _matmul(hin2, w_gate, j)
            y, c1, n1, m1 = _mlstm(proj, gates, b_gate[j:j + 1], m_norm_g[j], mc[j], mn[j], mm[j], batch, seq)
            mix = _matmul(y, m_w_out, j)
            out_c.append(c1)
            out_n.append(n1)
            out_m.append(m1)
        else:
            proj = _matmul(hin2, h_w_in, j)
            y, s1 = _hgrn(proj, h_lower_bounds, li, h_norm_g[j], hs[j], batch, seq)
            mix = _matmul(y, h_w_out, j)
            out_s.append(s1)
        x, hin = _res_ln(x, mix.reshape(batch, seq, d), mod, ln_g, ln_b, li, 0, seq_off, alpha, (li, 4, 3))
        ff = _peer(hin, p_w_query, p_sub_keys, uv_pack, li)
        nxt = (li + 1, 1, 0) if li + 1 < depth else None
        x, hin = _res_ln(x, ff.reshape(batch, seq, d), mod, ln_g, ln_b, li, 1, seq_off, alpha, nxt)
    return x, jnp.stack(out_c), jnp.stack(out_n), jnp.stack(out_m), jnp.stack(out_s)


def kernel(x_prompt, x_sample, c_prompt, c_sample, state_mlstm_C, state_mlstm_n, state_mlstm_m, state_hgrn_S, ada_w, ada_b, ln_g, ln_b, m_w_in, m_b_gate, m_norm_g, m_w_out, h_w_in, h_lower_bounds, h_norm_g, h_w_out, p_w_query, p_sub_keys, p_u, p_v):
    depth, d, _ = ada_w.shape
    n_a, m_heads, m_dv = m_norm_g.shape
    n_b = h_norm_g.shape[0]
    batch = x_prompt.shape[0]
    dec_batch = x_sample.shape[0]

    n_seq = dec_batch + batch
    pad = -n_seq % SUBLANES
    c_all = jnp.concatenate([c_sample, c_prompt, jnp.zeros((pad, d), F32)], axis=0)
    mod = _ada(c_all, ada_w, ada_b).reshape(depth, n_seq + pad, 1, 6 * d)

    n_main = m_w_in.shape[2] - 2 * m_heads
    w_gate = jnp.pad(m_w_in[:, :, n_main:], ((0, 0), (0, 0), (0, LANES - 2 * m_heads)))
    b_gate = jnp.pad(m_b_gate, ((0, 0), (0, LANES - 2 * m_heads)))
    uv_pack = _pack_tables(p_u, p_v)

    m_dk = state_mlstm_C.shape[3]
    _, _, h_heads, h_dk, h_dv = state_hgrn_S.shape
    z_c = jnp.zeros((n_a, batch, m_heads, m_dk, m_dv), F32)
    z_n = jnp.zeros((n_a, batch, m_heads, m_dk), F32)
    z_m = jnp.zeros((n_a, batch, m_heads), F32)
    z_s = jnp.zeros((n_b, batch, h_heads, h_dk, h_dv), F32)

    shared = (ln_g, ln_b, m_w_in, w_gate, b_gate, m_norm_g, m_w_out, h_w_in, h_lower_bounds, h_norm_g, h_w_out,
              p_w_query, p_sub_keys, uv_pack)
    y_p, p_c, p_n, p_m, p_s = _trunk(x_prompt, mod, dec_batch, z_c, z_n, z_m, z_s, *shared)
    y_s, s_c, s_n, s_m, s_s = _trunk(x_sample, mod, 0, state_mlstm_C, state_mlstm_n, state_mlstm_m, state_hgrn_S,
                                     *shared)
    return (y_p, y_s, p_c, p_n, p_m, p_s, s_c, s_n, s_m, s_s)
```
